```python
import numpy as np
import jax
import jax.numpy as jnp
from jax import lax


D_MODEL = 1024
BATCH = 2
SEQ = 8192
DEPTH = 4

GRID_W = 64
CTX_LEN = 256
ROPE_THETA = 10000.0
Q_BLOCK = 128
LN_EPS = 1e-6
RMS_EPS = 1e-6
DEEPNORM_ALPHA = (2 * DEPTH) ** 0.25
DEEPNORM_BETA = (8 * DEPTH) ** -0.25

GLA_HEADS = 4
GLA_DK = 32
GLA_DV = 64
GLA_GATE_RANK = 16
GLA_TAU = 16.0
GLA_CHUNK = 64
MLA_HEADS = 6
MLA_Q_RANK = 384
MLA_KV_RANK = 256
MLA_NOPE = 64
MLA_ROPE = 32
MLA_DV = 64
MLA_SCALE = (MLA_NOPE + MLA_ROPE) ** -0.5
GQA_HEADS = 6
GQA_KV_HEADS = 2
GQA_DH = 64
GQA_GROUP = GQA_HEADS // GQA_KV_HEADS
GQA_SCALE = GQA_DH ** -0.5
N_EXPERTS = 16
EC_CAPACITY_FACTOR = 2
D_EXPERT = 1024

GLA_WIDTH = GLA_HEADS * GLA_DV
MLA_WIDTH = MLA_HEADS * MLA_DV
GQA_WIDTH = GQA_HEADS * GQA_DH
MIX_WIDTH = GLA_WIDTH + MLA_WIDTH + GQA_WIDTH
IN_SPLITS = (GLA_HEADS * GLA_DK, GLA_HEADS * GLA_DK, GLA_WIDTH, GLA_WIDTH, GLA_GATE_RANK, GLA_GATE_RANK,
             MLA_Q_RANK, MLA_KV_RANK, MLA_ROPE,
             GQA_HEADS * GQA_DH, GQA_KV_HEADS * GQA_DH, GQA_KV_HEADS * GQA_DH)
IN_WIDTH = sum(IN_SPLITS)
IN_OFFSETS = [int(o) for o in np.cumsum(IN_SPLITS)[:-1]]

kernel_name = 'hybrid_gla_mla_gqa_ec_moe_diffusion_block'


def layer_norm(x, g, b):
    xf = x.astype(jnp.float32)
    mu = jnp.mean(xf, axis=-1, keepdims=True)
    var = jnp.mean(jnp.square(xf - mu), axis=-1, keepdims=True)
    return ((xf - mu) * lax.rsqrt(var + LN_EPS) * g.astype(jnp.float32) + b.astype(jnp.float32)).astype(x.dtype)


def rms_norm(x, g):
    xf = x.astype(jnp.float32)
    ms = jnp.mean(jnp.square(xf), axis=-1, keepdims=True)
    return (xf * lax.rsqrt(ms + RMS_EPS) * g.astype(jnp.float32)).astype(x.dtype)


def rope_1d(x, pos):
    half = x.shape[-1] // 2
    inv_freq = ROPE_THETA ** (-jnp.arange(half, dtype=jnp.float32) / half)
    ang = pos.astype(jnp.float32)[:, None] * inv_freq
    cos = jnp.cos(ang)[:, None, :]
    sin = jnp.sin(ang)[:, None, :]
    xf = x.astype(jnp.float32)
    x1, x2 = xf[..., :half], xf[..., half:]
    return jnp.concatenate([x1 * cos - x2 * sin, x1 * sin + x2 * cos], axis=-1).astype(x.dtype)


def rope_2d(x, row, col):
    half = x.shape[-1] // 2
    return jnp.concatenate([rope_1d(x[..., :half], row), rope_1d(x[..., half:], col)], axis=-1)


def blocked_attention(q, k, v, scale):
    B, Lq = q.shape[0], q.shape[1]
    nb = Lq // Q_BLOCK
    qb = q.reshape((B, nb, Q_BLOCK) + q.shape[2:]).swapaxes(0, 1)

    def one_block(q_blk):
        s = jnp.einsum('bqhgd,bkhd->bhgqk', q_blk, k).astype(jnp.float32) * scale
        p = jax.nn.softmax(s, axis=-1)
        return jnp.einsum('bhgqk,bkhe->bqhge', p.astype(v.dtype), v)

    ob = lax.map(one_block, qb)
    return ob.swapaxes(0, 1).reshape((B, Lq) + ob.shape[3:])


def gla_chunked(q, k, v, log_a, s0):
    B, L, H, _ = q.shape
    n = L // GLA_CHUNK
    f32 = jnp.float32

    def rs(t):
        return t.astype(f32).reshape(B, n, GLA_CHUNK, H, t.shape[-1])

    qc, kc, vc, la = rs(q), rs(k), rs(v), rs(log_a)
    b = jnp.cumsum(la, axis=2)
    b_last = b[:, :, -1:]
    q_dec = qc * jnp.exp(b)
    k_inv = kc * jnp.exp(-b)
    k_end = kc * jnp.exp(b_last - b)
    mask = jnp.tril(jnp.ones((GLA_CHUNK, GLA_CHUNK), f32))
    attn = jnp.einsum('bnihd,bnjhd->bnhij', q_dec, k_inv) * mask
    o_intra = jnp.einsum('bnhij,bnjhe->bnihe', attn, vc)
    u = jnp.einsum('bnjhd,bnjhe->bnhde', k_end, vc)
    decay = jnp.exp(b_last[:, :, 0])

    def step(s, inp):
        dec, uc = inp
        return dec[..., None] * s + uc, s

    _, s_prev = lax.scan(step, s0.astype(f32), (decay.swapaxes(0, 1), u.swapaxes(0, 1)))
    o_inter = jnp.einsum('bnihd,nbhde->bnihe', q_dec, s_prev)
    return (o_intra + o_inter).reshape(B, L, H, v.shape[-1]).astype(v.dtype)


def gla_final_state(k, v, log_a):
    b = jnp.cumsum(log_a.astype(jnp.float32), axis=1)
    w = jnp.exp(b[:, -1:] - b)
    return jnp.einsum('blhd,blhe->bhde', k.astype(jnp.float32) * w, v.astype(jnp.float32))


def flip_seq(t):
    return jnp.flip(t, axis=1)


def gla_bidirectional(g, s_f, s_b):
    o_f = gla_chunked(g['q'], g['k'], g['v'], g['la_f'], s_f)
    o_b = flip_seq(gla_chunked(flip_seq(g['q']), flip_seq(g['k']), flip_seq(g['v']), flip_seq(g['la_b']), s_b))
    return o_f + o_b


def project_heads(h, pos, w_in, w_gla_gate_f, b_gla_gate_f, w_gla_gate_b, b_gla_gate_b,
                  g_mla_qa, w_mla_qb, g_mla_kva, w_mla_kvb, g_gqa_q, g_gqa_k):
    B, L, _ = h.shape
    (gq, gk, gv, gr, glr_f, glr_b, mqa, mkva, mkr, aq, ak, av) = jnp.split(h @ w_in, IN_OFFSETS, axis=-1)
    f32 = jnp.float32
    la_f = jax.nn.log_sigmoid((glr_f @ w_gla_gate_f + b_gla_gate_f).astype(f32)) / GLA_TAU
    la_b = jax.nn.log_sigmoid((glr_b @ w_gla_gate_b + b_gla_gate_b).astype(f32)) / GLA_TAU
    gla = {
        'q': gq.reshape(B, L, GLA_HEADS, GLA_DK) * (GLA_DK ** -0.5),
        'k': gk.reshape(B, L, GLA_HEADS, GLA_DK),
        'v': gv.reshape(B, L, GLA_HEADS, GLA_DV),
        'r': gr,
        'la_f': la_f.reshape(B, L, GLA_HEADS, GLA_DK),
        'la_b': la_b.reshape(B, L, GLA_HEADS, GLA_DK),
    }
    q_m = (rms_norm(mqa, g_mla_qa) @ w_mla_qb).reshape(B, L, MLA_HEADS, MLA_NOPE + MLA_ROPE)
    q_nope, q_rope = q_m[..., :MLA_NOPE], q_m[..., MLA_NOPE:]
    kv_m = (rms_norm(mkva, g_mla_kva) @ w_mla_kvb).reshape(B, L, MLA_HEADS, MLA_NOPE + MLA_DV)
    k_nope, v_m = kv_m[..., :MLA_NOPE], kv_m[..., MLA_NOPE:]
    k_rope = mkr.reshape(B, L, 1, MLA_ROPE)
    q_g = rms_norm(aq.reshape(B, L, GQA_HEADS, GQA_DH), g_gqa_q)
    k_g = rms_norm(ak.reshape(B, L, GQA_KV_HEADS, GQA_DH), g_gqa_k)
    v_g = av.reshape(B, L, GQA_KV_HEADS, GQA_DH)
    if pos is not None:
        row, col = pos
        q_rope = rope_2d(q_rope, row, col)
        k_rope = rope_2d(k_rope, row, col)
        q_g = rope_2d(q_g, row, col)
        k_g = rope_2d(k_g, row, col)
    mla = {
        'q': jnp.concatenate([q_nope, q_rope], axis=-1)[:, :, :, None, :],
        'k': jnp.concatenate([k_nope, jnp.broadcast_to(k_rope, (B, L, MLA_HEADS, MLA_ROPE))], axis=-1),
        'v': v_m,
    }
    gqa = {'q': q_g.reshape(B, L, GQA_KV_HEADS, GQA_GROUP, GQA_DH), 'k': k_g, 'v': v_g}
    return gla, mla, gqa


def merge_heads(o_gla, r, g_gla_norm, o_mla, o_gqa, w_out):
    B, L = o_gla.shape[0], o_gla.shape[1]
    gla = rms_norm(o_gla, g_gla_norm).reshape(B, L, GLA_WIDTH) * jax.nn.silu(r)
    cat = jnp.concatenate([gla, o_mla.reshape(B, L, MLA_WIDTH), o_gqa.reshape(B, L, GQA_WIDTH)], axis=-1)
    return cat @ w_out


def token_mixer(h_lat, h_ctx, pos, w_in, w_gla_gate_f, b_gla_gate_f, w_gla_gate_b, b_gla_gate_b,
                g_gla_norm, g_mla_qa, w_mla_qb, g_mla_kva, w_mla_kvb, g_gqa_q, g_gqa_k, w_out, need_ctx):
    proj = (w_in, w_gla_gate_f, b_gla_gate_f, w_gla_gate_b, b_gla_gate_b,
            g_mla_qa, w_mla_qb, g_mla_kva, w_mla_kvb, g_gqa_q, g_gqa_k)
    gla_l, mla_l, gqa_l = project_heads(h_lat, pos, *proj)
    gla_c, mla_c, gqa_c = project_heads(h_ctx, None, *proj)
    s_f = gla_final_state(gla_c['k'], gla_c['v'], gla_c['la_f'])
    s_b = gla_final_state(flip_seq(gla_c['k']), flip_seq(gla_c['v']), flip_seq(gla_c['la_b']))
    o_gla = gla_bidirectional(gla_l, s_f, s_b)

    def cat(a, b):
        return jnp.concatenate([a, b], axis=1)

    o_mla = blocked_attention(mla_l['q'], cat(mla_c['k'], mla_l['k']), cat(mla_c['v'], mla_l['v']), MLA_SCALE)
    o_gqa = blocked_attention(gqa_l['q'], cat(gqa_c['k'], gqa_l['k']), cat(gqa_c['v'], gqa_l['v']), GQA_SCALE)
    y_lat = merge_heads(o_gla, gla_l['r'], g_gla_norm, o_mla, o_gqa, w_out)
    if not need_ctx:
        return y_lat, None
    zeros = jnp.zeros_like(s_f)
    o_gla_c = gla_bidirectional(gla_c, zeros, zeros)
    o_mla_c = blocked_attention(mla_c['q'], mla_c['k'], mla_c['v'], MLA_SCALE)
    o_gqa_c = blocked_attention(gqa_c['q'], gqa_c['k'], gqa_c['v'], GQA_SCALE)
    y_ctx = merge_heads(o_gla_c, gla_c['r'], g_gla_norm, o_mla_c, o_gqa_c, w_out)
    return y_lat, y_ctx


def expert_choice_ffn(h, w_router, w_gate, w_up, w_down):
    B, n, D = h.shape
    cap = EC_CAPACITY_FACTOR * n // N_EXPERTS
    aff = jax.nn.softmax((h @ w_router).astype(jnp.float32), axis=-1)
    gate, idx = lax.top_k(aff.swapaxes(1, 2), cap)
    xs = jax.vmap(lambda hb, ib: hb[ib])(h, idx)
    hid = jax.nn.silu(jnp.einsum('becd,edf->becf', xs, w_gate)) * jnp.einsum('becd,edf->becf', xs, w_up)
    out = jnp.einsum('becf,efd->becd', hid, w_down) * gate[..., None].astype(h.dtype)
    return jax.vmap(lambda ib, ob: jnp.zeros((n, D), ob.dtype).at[ib.reshape(-1)].add(ob.reshape(-1, D)))(idx, out)


def setup_inputs(seed: int = 0) -> dict:
    key = jax.random.key(seed)
    ks = jax.random.split(key, 32)
    f32 = jnp.float32

    def nrm(i, shape, scale):
        return jax.random.normal(ks[i], shape, f32) * scale

    def gain(i, shape):
        return 1.0 + nrm(i, shape, 0.02)

    return {
        'x': nrm(0, (BATCH, SEQ, D_MODEL), 1.0),
        'c': nrm(1, (BATCH, D_MODEL), 1.0),
        'ctx': nrm(2, (BATCH, CTX_LEN, D_MODEL), 1.0),
        'c_ctx': nrm(3, (D_MODEL,), 1.0),
        'w_mod': nrm(4, (DEPTH, D_MODEL, 6 * D_MODEL), 0.5 * D_MODEL ** -0.5),
        'b_mod': nrm(5, (DEPTH, 6 * D_MODEL), 0.02),
        'w_in': nrm(6, (DEPTH, D_MODEL, IN_WIDTH), D_MODEL ** -0.5),
        'w_gla_gate_f': nrm(7, (DEPTH, GLA_GATE_RANK, GLA_HEADS * GLA_DK), GLA_GATE_RANK ** -0.5),
        'b_gla_gate_f': nrm(8, (DEPTH, GLA_HEADS * GLA_DK), 0.5),
        'w_gla_gate_b': nrm(9, (DEPTH, GLA_GATE_RANK, GLA_HEADS * GLA_DK), GLA_GATE_RANK ** -0.5),
        'b_gla_gate_b': nrm(10, (DEPTH, GLA_HEADS * GLA_DK), 0.5),
        'g_gla_norm': gain(11, (DEPTH, GLA_DV)),
        'g_mla_qa': gain(12, (DEPTH, MLA_Q_RANK)),
        'w_mla_qb': nrm(13, (DEPTH, MLA_Q_RANK, MLA_HEADS * (MLA_NOPE + MLA_ROPE)), MLA_Q_RANK ** -0.5),
        'g_mla_kva': gain(14, (DEPTH, MLA_KV_RANK)),
        'w_mla_kvb': nrm(15, (DEPTH, MLA_KV_RANK, MLA_HEADS * (MLA_NOPE + MLA_DV)), MLA_KV_RANK ** -0.5),
        'g_gqa_q': gain(16, (DEPTH, GQA_DH)),
        'g_gqa_k': gain(17, (DEPTH, GQA_DH)),
        'w_out': nrm(18, (DEPTH, MIX_WIDTH, D_MODEL), DEEPNORM_BETA * MIX_WIDTH ** -0.5),
        'ln1_g': gain(19, (DEPTH, D_MODEL)),
        'ln1_b': nrm(20, (DEPTH, D_MODEL), 0.02),
        'w_router': nrm(21, (DEPTH, D_MODEL, N_EXPERTS), D_MODEL ** -0.5),
        'w_exp_gate': nrm(22, (DEPTH, N_EXPERTS, D_MODEL, D_EXPERT), D_MODEL ** -0.5),
        'w_exp_up': nrm(23, (DEPTH, N_EXPERTS, D_MODEL, D_EXPERT), D_MODEL ** -0.5),
        'w_exp_down': nrm(24, (DEPTH, N_EXPERTS, D_EXPERT, D_MODEL), DEEPNORM_BETA * D_EXPERT ** -0.5),
        'ln2_g': gain(25, (DEPTH, D_MODEL)),
        'ln2_b': nrm(26, (DEPTH, D_MODEL), 0.02),
    }


def reference(x, c, ctx, c_ctx, w_mod, b_mod, w_in, w_gla_gate_f, b_gla_gate_f, w_gla_gate_b, b_gla_gate_b,
              g_gla_norm, g_mla_qa, w_mla_qb, g_mla_kva, w_mla_kvb, g_gqa_q, g_gqa_k, w_out, ln1_g, ln1_b,
              w_router, w_exp_gate, w_exp_up, w_exp_down, ln2_g, ln2_b):
    L = x.shape[1]
    rows = L // GRID_W
    row = jnp.repeat(jnp.arange(rows, dtype=jnp.int32), GRID_W)
    col = jnp.arange(L, dtype=jnp.int32) % GRID_W
    pos = (row, col)
    silu_c = jax.nn.silu(c)
    silu_cc = jax.nn.silu(c_ctx)
    cx = ctx
    for l in range(DEPTH):
        need_ctx = l < DEPTH - 1
        mod = (silu_c @ w_mod[l] + b_mod[l])[:, None, :]
        mod_c = silu_cc @ w_mod[l] + b_mod[l]
        sh1, sc1, g1, sh2, sc2, g2 = jnp.split(mod, 6, axis=-1)
        csh1, csc1, cg1, csh2, csc2, cg2 = jnp.split(mod_c, 6, axis=-1)
        h_lat = x * (1.0 + sc1) + sh1
        h_ctx = cx * (1.0 + csc1) + csh1
        a_lat, a_ctx = token_mixer(h_lat, h_ctx, pos, w_in[l], w_gla_gate_f[l], b_gla_gate_f[l],
                                   w_gla_gate_b[l], b_gla_gate_b[l], g_gla_norm[l], g_mla_qa[l], w_mla_qb[l],
                                   g_mla_kva[l], w_mla_kvb[l], g_gqa_q[l], g_gqa_k[l], w_out[l], need_ctx)
        x = layer_norm(DEEPNORM_ALPHA * x + g1 * a_lat, ln1_g[l], ln1_b[l])
        f_lat = expert_choice_ffn(x * (1.0 + sc2) + sh2, w_router[l], w_exp_gate[l], w_exp_up[l], w_exp_down[l])
        x = layer_norm(DEEPNORM_ALPHA * x + g2 * f_lat, ln2_g[l], ln2_b[l])
        if need_ctx:
            cx = layer_norm(DEEPNORM_ALPHA * cx + cg1 * a_ctx, ln1_g[l], ln1_b[l])
            f_ctx = expert_choice_ffn(cx * (1.0 + csc2) + csh2, w_router[l], w_exp_gate[l], w_exp_up[l], w_exp_down[l])
            cx = layer_norm(DEEPNORM_ALPHA * cx + cg2 * f_ctx, ln2_g[l], ln2_b[l])
    return x
```

```python
import functools
import math

import numpy as np
import jax
import jax.numpy as jnp
from jax import lax
from jax.experimental import pallas as pl
from jax.experimental.pallas import tpu as pltpu

F32 = jnp.float32
BF16 = jnp.bfloat16

D_MODEL = 1024
DEPTH = 4
GRID_W = 64
ROPE_THETA = 10000.0
LN_EPS = 1e-6
RMS_EPS = 1e-6
DEEPNORM_ALPHA = (2 * DEPTH) ** 0.25

GLA_HEADS, GLA_DK, GLA_DV, GLA_GATE_RANK, GLA_TAU, GLA_CHUNK = 4, 32, 64, 16, 16.0, 64
MLA_HEADS, MLA_Q_RANK, MLA_KV_RANK, MLA_NOPE, MLA_ROPE, MLA_DV = 6, 384, 256, 64, 32, 64
MLA_SCALE = (MLA_NOPE + MLA_ROPE) ** -0.5
GQA_HEADS, GQA_KV_HEADS, GQA_DH = 6, 2, 64
GQA_GROUP = GQA_HEADS // GQA_KV_HEADS
GQA_SCALE = GQA_DH ** -0.5
N_EXPERTS = 16
EC_CAPACITY_FACTOR = 2
D_EXPERT = 1024

GLA_WIDTH = GLA_HEADS * GLA_DV
IN_SPLITS = (GLA_HEADS * GLA_DK, GLA_HEADS * GLA_DK, GLA_WIDTH, GLA_WIDTH, GLA_GATE_RANK, GLA_GATE_RANK,
             MLA_Q_RANK, MLA_KV_RANK, MLA_ROPE,
             GQA_HEADS * GQA_DH, GQA_KV_HEADS * GQA_DH, GQA_KV_HEADS * GQA_DH)
IN_OFF = [0] + [int(o) for o in np.cumsum(IN_SPLITS)]

LANE = 128
HEAD_PAD = 128
N_QHEADS = MLA_HEADS + GQA_HEADS
N_KVHEADS = MLA_HEADS + GQA_KV_HEADS
ATT_DV = 64
LOG2E = math.log2(math.e)
VMEM_LIMIT = 56 * 1024 * 1024

C_A, C_MQA, C_MKVA, C_AQ, C_AK, C_AV, C_G, C_AQS, C_AKS, C_GS, C_END = (
    0, 768, 1152, 1408, 2176, 2432, 2560, 2688, 3456, 3712, 3840)


def _cparams(sem):
    return pltpu.CompilerParams(dimension_semantics=sem, vmem_limit_bytes=VMEM_LIMIT)


def _dot(a, b):
    return jnp.dot(a, b, preferred_element_type=F32)


def _dot_nt(a, b):
    return lax.dot_general(a, b, (((1,), (1,)), ((), ())), preferred_element_type=F32)


def _dot_tn(a, b):
    return lax.dot_general(a, b, (((0,), (0,)), ((), ())), preferred_element_type=F32)


def _split2(x):
    hi = x.astype(BF16)
    lo = (x - hi.astype(F32)).astype(BF16)
    return hi, lo


def _split3(x):
    hi = x.astype(BF16)
    r = x - hi.astype(F32)
    mid = r.astype(BF16)
    lo = (r - mid.astype(F32)).astype(BF16)
    return hi, mid, lo


def _layer_norm(z, g, b):
    mu = jnp.mean(z, axis=-1, keepdims=True)
    zc = z - mu
    var = jnp.mean(zc * zc, axis=-1, keepdims=True)
    return zc * lax.rsqrt(var + LN_EPS) * g + b


def _mod_kernel(c_ref, w_ref, b_ref, o_ref):
    cv = c_ref[...]
    s = cv * (1.0 / (1.0 + jnp.exp(-cv)))
    o_ref[0] = _dot(s.astype(BF16), w_ref[0].astype(BF16)) + b_ref[0]


def _mod_call(cvec, w_mod, b_mod):
    depth, d, n6 = w_mod.shape
    bn = 768
    return pl.pallas_call(
        _mod_kernel,
        grid=(depth, n6 // bn),
        in_specs=[pl.BlockSpec((8, d), lambda l, j: (0, 0)),
                  pl.BlockSpec((1, d, bn), lambda l, j: (l, 0, j)),
                  pl.BlockSpec((1, 1, bn), lambda l, j: (l, 0, j))],
        out_specs=pl.BlockSpec((1, 8, bn), lambda l, j: (l, 0, j)),
        out_shape=jax.ShapeDtypeStruct((depth, 8, n6), F32),
        compiler_params=_cparams(("arbitrary", "arbitrary")),
    )(cvec, w_mod, b_mod.reshape(depth, 1, n6))


def _proj_kernel(x_ref, sc_ref, sh_ref, tab_ref, w_ref, wg_ref, bg_ref, gqa_ref, wqb_ref, gkva_ref, wkvb_ref,
                 gains_ref, bsum_ref,
                 gq_ref, gk_ref, la_ref, gv_ref, gr_ref, q_ref, k_ref, v_ref):
    h = x_ref[0] * (1.0 + sc_ref[0]) + sh_ref[0]
    hb = h.astype(BF16)

    def mm(a, b):
        return _dot(hb, w_ref[:, a:b])

    def tab(i):
        return tab_ref[:, i * LANE:(i + 1) * LANE]

    a = mm(C_A, C_MQA)
    gq_ref[0] = a[:, 0:128] * (GLA_DK ** -0.5)
    gk_ref[0] = a[:, 128:256]
    gv_ref[0] = a[:, 256:512]
    gr_ref[0] = a[:, 512:768]

    g = mm(C_G, C_AQS)
    gs = mm(C_GS, C_END)
    lg = _dot(g.astype(BF16), wg_ref[...]) + bg_ref[...]
    la_ref[0] = (jnp.minimum(lg, 0.0) - jnp.log1p(jnp.exp(-jnp.abs(lg)))) * (1.0 / GLA_TAU)

    mqa = mm(C_MQA, C_MKVA)
    nq = mqa * lax.rsqrt(jnp.mean(mqa * mqa, axis=-1, keepdims=True) + RMS_EPS) * gqa_ref[...]
    q2 = _dot(nq.astype(BF16), wqb_ref[...])
    cm, sm = tab(0), tab(1)
    for hh in range(MLA_HEADS):
        lo = hh * HEAD_PAD
        qh = q2[:, lo:lo + HEAD_PAD] * cm + q2[:, 768 + lo:768 + lo + HEAD_PAD] * sm
        q_ref[0, :, lo:lo + HEAD_PAD] = qh.astype(BF16)

    mkva = mm(C_MKVA, C_AQ)
    nkv = mkva * lax.rsqrt(jnp.mean(mkva * mkva, axis=-1, keepdims=True) + RMS_EPS) * gkva_ref[...]
    kv = _dot(nkv.astype(BF16), wkvb_ref[...])
    kr = g * tab(2) + gs * tab(3)
    for hh in range(MLA_HEADS):
        lo = hh * HEAD_PAD
        k_ref[0, :, lo:lo + HEAD_PAD] = (kv[:, lo:lo + HEAD_PAD] + kr).astype(BF16)
    v_ref[0, :, 0:384] = kv[:, 768:1152].astype(BF16)

    def head_rs(t, width):
        hi, lo = _split2(t * t)
        bs = bsum_ref[0:width, 0:width]
        ms = (_dot(hi, bs) + _dot(lo, bs)) * (1.0 / GQA_DH)
        return lax.rsqrt(ms + RMS_EPS)

    aq = mm(C_AQ, C_AK)
    aqs = mm(C_AQS, C_AKS)
    rs = head_rs(aq, 768)
    gq_g, gq_s = gains_ref[0:1, 0:768], gains_ref[1:2, 0:768]
    cg, sg = tab(4), tab(5)
    for hh in range(GQA_HEADS):
        lo = hh * HEAD_PAD
        sl = slice(lo, lo + HEAD_PAD)
        qh = aq[:, sl] * rs[:, sl] * gq_g[:, sl] * cg + aqs[:, sl] * rs[:, sl] * gq_s[:, sl] * sg
        q_ref[0, :, 768 + lo:768 + lo + HEAD_PAD] = qh.astype(BF16)

    ak = mm(C_AK, C_AV)
    aks = mm(C_AKS, C_GS)
    rsk = head_rs(ak, 256)
    gk_g, gk_s = gains_ref[2:3, 0:256], gains_ref[3:4, 0:256]
    ck, sk = tab(6), tab(7)
    for hh in range(GQA_KV_HEADS):
        lo = hh * HEAD_PAD
        sl = slice(lo, lo + HEAD_PAD)
        kh = ak[:, sl] * rsk[:, sl] * gk_g[:, sl] * ck + aks[:, sl] * rsk[:, sl] * gk_s[:, sl] * sk
        k_ref[0, :, 768 + lo:768 + lo + HEAD_PAD] = kh.astype(BF16)
    v_ref[0, :, 384:512] = mm(C_AV, C_G).astype(BF16)


def _proj_call(x, sc, sh, tab, w):
    b, l, d = x.shape
    t = min(l, 256)
    full = lambda arr: pl.BlockSpec(arr.shape, lambda bi, i: (0,) * arr.ndim)
    row = lambda n: pl.BlockSpec((1, t, n), lambda bi, i: (bi, i, 0))
    vec = pl.BlockSpec((1, 1, d), lambda bi, i: (bi, 0, 0))
    consts = [w['w_in'], w['wg'], w['bg'], w['g_qa'], w['wqb'], w['g_kva'], w['wkvb'], w['gains'], w['bsum']]
    outs = [(128, F32), (128, F32), (256, F32), (256, F32), (256, F32), (1536, BF16), (1024, BF16), (512, BF16)]
    return pl.pallas_call(
        _proj_kernel,
        grid=(b, l // t),
        in_specs=[row(d), vec, vec, pl.BlockSpec((t, 8 * LANE), lambda bi, i: (i, 0))] + [full(a) for a in consts],
        out_specs=[row(n) for n, _ in outs],
        out_shape=[jax.ShapeDtypeStruct((b, l, n), dt) for n, dt in outs],
        compiler_params=_cparams(("arbitrary", "arbitrary")),
    )(x, sc, sh, tab, *consts)


GLA_BLOCK = 256


def _gla_kernel(q_ref, k_ref, la_ref, v_ref, s0_ref, o_ref, sfin_ref, s_scr, *, reverse, nblk):
    i = pl.program_id(1)

    @pl.when(i == 0)
    def _():
        s_scr[...] = s0_ref[0]

    r = q_ref.shape[1]
    nch = r // GLA_CHUNK
    la = la_ref[0]
    q = q_ref[0]
    k = k_ref[0]
    v = v_ref[0]
    ri = lax.broadcasted_iota(jnp.int32, (r, r), 0)
    ci = lax.broadcasted_iota(jnp.int32, (r, r), 1)
    same = (ri >> 6) == (ci >> 6)
    tri = jnp.logical_and(same, (ci >= ri) if reverse else (ci <= ri))
    trib = jnp.where(tri, 1.0, 0.0).astype(BF16)
    oneb = jnp.where(same, 1.0, 0.0).astype(BF16)
    hi, mid, lo = _split3(la)
    bcum = _dot(trib, hi) + _dot(trib, mid) + _dot(trib, lo)
    tot = _dot(oneb, hi) + _dot(oneb, mid) + _dot(oneb, lo)
    qd = q * jnp.exp(bcum)
    kinv = (k * jnp.exp(-bcum)).astype(BF16)
    kend = (k * jnp.exp(tot - bcum)).astype(BF16)
    dec = jnp.exp(tot)
    vb = v.astype(BF16)
    lane_k = lax.broadcasted_iota(jnp.int32, (1, GLA_HEADS * GLA_DK), 1)
    lane_v = lax.broadcasted_iota(jnp.int32, (1, GLA_WIDTH), 1)
    o = jnp.zeros((r, GLA_WIDTH), F32)
    for hh in range(GLA_HEADS):
        qh = jnp.where((lane_k >> 5) == hh, qd, 0.0).astype(BF16)
        att = jnp.where(tri, _dot_nt(qh, kinv), 0.0)
        vh = jnp.where((lane_v >> 6) == hh, v, 0.0).astype(BF16)
        o = o + _dot(att.astype(BF16), vh)
    qdb = qd.astype(BF16)
    bd = (lax.broadcasted_iota(jnp.int32, (GLA_WIDTH, GLA_HEADS * GLA_DK), 0) >> 6) == (
        lax.broadcasted_iota(jnp.int32, (GLA_WIDTH, GLA_HEADS * GLA_DK), 1) >> 5)
    s = s_scr[...]
    order = range(nch - 1, -1, -1) if reverse else range(nch)
    for cc in order:
        rows = slice(cc * GLA_CHUNK, (cc + 1) * GLA_CHUNK)
        o_ref[0, rows, :] = o[rows] + _dot_nt(qdb[rows], s.astype(BF16))
        ut = _dot_tn(vb[rows], kend[rows])
        s = dec[cc * GLA_CHUNK:cc * GLA_CHUNK + 1, :] * s + jnp.where(bd, ut, 0.0)
    s_scr[...] = s

    @pl.when(i == nblk - 1)
    def _():
        sfin_ref[0] = s


def _gla_call(p, s0, direction):
    b, l, _ = p['gq'].shape
    r = min(l, GLA_BLOCK)
    nblk = l // r
    reverse = direction == 1
    blk = (lambda i: nblk - 1 - i) if reverse else (lambda i: i)
    row = lambda n, col=0: pl.BlockSpec((1, r, n), lambda bi, i: (bi, blk(i), col))
    st = pl.BlockSpec((1, GLA_WIDTH, 128), lambda bi, i: (bi, 0, 0))
    return pl.pallas_call(
        functools.partial(_gla_kernel, reverse=reverse, nblk=nblk),
        grid=(b, nblk),
        in_specs=[row(128), row(128), row(128, direction), row(256), st],
        out_specs=[row(256), st],
        out_shape=[jax.ShapeDtypeStruct((b, l, GLA_WIDTH), F32), jax.ShapeDtypeStruct((b, GLA_WIDTH, 128), F32)],
        scratch_shapes=[pltpu.VMEM((GLA_WIDTH, 128), F32)],
        compiler_params=_cparams(("arbitrary", "arbitrary")),
    )(p['gq'], p['gk'], p['la'], p['gv'], s0)


def _attn_kernel(qt_ref, k_ref, vt_ref, o_ref, m_scr, l_scr, acc_scr, *, nk):
    ki = pl.program_id(3)

    @pl.when(ki == 0)
    def _():
        m_scr[...] = jnp.full(m_scr.shape, -jnp.inf, F32)
        l_scr[...] = jnp.zeros(l_scr.shape, F32)
        acc_scr[...] = jnp.zeros(acc_scr.shape, F32)

    s = _dot(k_ref[0, 0], qt_ref[0, 0])
    m_prev = m_scr[...]
    m_new = jnp.maximum(m_prev, jnp.max(s, axis=0, keepdims=True))
    alpha = jnp.exp2(m_prev - m_new)
    p = jnp.exp2(s - m_new)
    l_scr[...] = alpha * l_scr[...] + jnp.sum(p, axis=0, keepdims=True)
    acc_scr[...] = alpha * acc_scr[...] + _dot(vt_ref[0, 0], p.astype(BF16))
    m_scr[...] = m_new

    @pl.when(ki == nk - 1)
    def _():
        o_ref[0, 0] = (acc_scr[...] / l_scr[...]).astype(o_ref.dtype)


def _pick_tile(n, options):
    for t in options:
        if n % t == 0:
            return t
    raise ValueError(f"no tile for {n}")


def _attn_call(qt, k, vt):
    b, nh, dk, lq = qt.shape
    lk = k.shape[2]
    tq = _pick_tile(lq, (1024, 512, 256))
    tk = _pick_tile(lk, (768, 512, 256))
    nk = lk // tk

    def kvh(h):
        return jnp.where(h < MLA_HEADS, h, MLA_HEADS + (h - MLA_HEADS) // GQA_GROUP)

    return pl.pallas_call(
        functools.partial(_attn_kernel, nk=nk),
        grid=(b, nh, lq // tq, nk),
        in_specs=[pl.BlockSpec((1, 1, dk, tq), lambda bi, h, qi, ki: (bi, h, 0, qi)),
                  pl.BlockSpec((1, 1, tk, dk), lambda bi, h, qi, ki: (bi, kvh(h), ki, 0)),
                  pl.BlockSpec((1, 1, ATT_DV, tk), lambda bi, h, qi, ki: (bi, kvh(h), 0, ki))],
        out_specs=pl.BlockSpec((1, 1, ATT_DV, tq), lambda bi, h, qi, ki: (bi, h, 0, qi)),
        out_shape=jax.ShapeDtypeStruct((b, nh, ATT_DV, lq), BF16),
        scratch_shapes=[pltpu.VMEM((1, tq), F32), pltpu.VMEM((1, tq), F32), pltpu.VMEM((ATT_DV, tq), F32)],
        compiler_params=_cparams(("arbitrary", "arbitrary", "arbitrary", "arbitrary")),
    )(qt, k, vt)


def _merge_kernel(x_ref, of_ref, ob_ref, gr_ref, oa_ref, g1_ref, sc_ref, sh_ref, ggla_ref, bs_ref, wout_ref,
                  lng_ref, lnb_ref, wrh_ref, wrl_ref, x1_ref, h2_ref, aff_ref):
    o = of_ref[0] + ob_ref[0]
    hi, lo = _split2(o * o)
    ms = (_dot(hi, bs_ref[...]) + _dot(lo, bs_ref[...])) * (1.0 / GLA_DV)
    r = gr_ref[0]
    gla = o * lax.rsqrt(ms + RMS_EPS) * ggla_ref[...] * (r * (1.0 / (1.0 + jnp.exp(-r))))
    a = _dot(gla.astype(BF16), wout_ref[0:GLA_WIDTH, :]) + _dot(oa_ref[0], wout_ref[GLA_WIDTH:, :])
    x1 = _layer_norm(DEEPNORM_ALPHA * x_ref[0] + g1_ref[0] * a, lng_ref[...], lnb_ref[...])
    x1_ref[0] = x1
    h2 = x1 * (1.0 + sc_ref[0]) + sh_ref[0]
    h2h, h2l = _split2(h2)
    h2_ref[0] = h2h
    wh, wl = wrh_ref[...], wrl_ref[...]
    logits = _dot_nt(wh, h2h) + _dot_nt(wh, h2l) + _dot_nt(wl, h2h)
    e = jnp.exp(logits - jnp.max(logits, axis=0, keepdims=True))
    aff_ref[0] = e / jnp.sum(e, axis=0, keepdims=True)


def _merge_call(x, of, ob, gr, oatt, g1, sc2, sh2, w):
    b, l, d = x.shape
    t = min(l, 256)
    full = lambda arr: pl.BlockSpec(arr.shape, lambda bi, i: (0,) * arr.ndim)
    row = lambda n: pl.BlockSpec((1, t, n), lambda bi, i: (bi, i, 0))
    vec = pl.BlockSpec((1, 1, d), lambda bi, i: (bi, 0, 0))
    consts = [w['g_gla'], w['bsum64'], w['w_out'], w['ln1_g'], w['ln1_b'], w['wr_hi'], w['wr_lo']]
    return pl.pallas_call(
        _merge_kernel,
        grid=(b, l // t),
        in_specs=[row(d), row(256), row(256), row(256), row(768), vec, vec, vec] + [full(a) for a in consts],
        out_specs=[row(d), row(d), pl.BlockSpec((1, N_EXPERTS, t), lambda bi, i: (bi, 0, i))],
        out_shape=[jax.ShapeDtypeStruct((b, l, d), F32), jax.ShapeDtypeStruct((b, l, d), BF16),
                   jax.ShapeDtypeStruct((b, N_EXPERTS, l), F32)],
        compiler_params=_cparams(("arbitrary", "arbitrary")),
    )(x, of, ob, gr, oatt, g1, sc2, sh2, *consts)


def _topk_kernel(aff_ref, slot_ref, off_ref, *, nc, cap):
    shift = int(math.log2(nc))
    bits = pltpu.bitcast(aff_ref[0], jnp.int32)
    r = bits.shape[0]
    ri = lax.broadcasted_iota(jnp.int32, (r, r), 0)
    ci = lax.broadcasted_iota(jnp.int32, (r, r), 1)
    same = (ri >> shift) == (ci >> shift)
    ones_bd = jnp.where(same, 1.0, 0.0).astype(BF16)
    lstrict = jnp.where(jnp.logical_and(same, ci < ri), 1.0, 0.0).astype(BF16)
    li = lax.broadcasted_iota(jnp.int32, (LANE, LANE), 0)
    lj = lax.broadcasted_iota(jnp.int32, (LANE, LANE), 1)
    uex = jnp.where(li < lj, 1.0, 0.0).astype(BF16)
    ones = jnp.ones((LANE, LANE), BF16)

    def total(mask):
        colsum = _dot(ones_bd, jnp.where(mask, 1.0, 0.0).astype(BF16))
        return _dot(colsum.astype(BF16), ones)

    def excl_cumsum(mask):
        mb = jnp.where(mask, 1.0, 0.0).astype(BF16)
        off = _dot(lstrict, _dot(mb, ones).astype(BF16))
        return _dot(mb, uex) + off, off

    def body(it, v):
        cand = v | lax.shift_left(jnp.int32(1), jnp.int32(30) - it)
        return jnp.where(total(bits >= cand) >= cap, cand, v)

    v = lax.fori_loop(0, 31, body, jnp.zeros(bits.shape, jnp.int32))
    gt = bits > v
    eq = bits == v
    need = cap - total(gt)
    rank_eq, _ = excl_cumsum(eq)
    sel = jnp.logical_or(gt, jnp.logical_and(eq, rank_eq < need))
    pos, off = excl_cumsum(sel)
    slot_ref[0] = jnp.where(sel, pos.astype(jnp.int32), -1)
    off_ref[0] = off.astype(jnp.int32)


def _topk_call(aff):
    b, e, l = aff.shape
    nc = l // LANE
    cap = EC_CAPACITY_FACTOR * l // N_EXPERTS
    r = e * nc
    spec = pl.BlockSpec((1, r, LANE), lambda bi: (bi, 0, 0))
    return pl.pallas_call(
        functools.partial(_topk_kernel, nc=nc, cap=cap),
        grid=(b,),
        in_specs=[spec],
        out_specs=[spec, spec],
        out_shape=[jax.ShapeDtypeStruct((b, r, LANE), jnp.int32)] * 2,
        compiler_params=_cparams(("arbitrary",)),
    )(aff.reshape(b, r, LANE))


def _expert_kernel(s0_ref, h_ref, slot_ref, aff_ref, wg_ref, wu_ref, wd_ref, ohi_ref, olo_ref, xs_scr, gate_scr,
                   *, nb, nsub, nc, cap, win):
    e, b, j = pl.program_id(0), pl.program_id(1), pl.program_id(2)

    @pl.when(j == 0)
    def _():
        xs_scr[...] = jnp.zeros(xs_scr.shape, F32)
        gate_scr[...] = jnp.zeros(gate_scr.shape, F32)

    base = (b * N_EXPERTS + e) * nc + j * nsub
    rows = lax.broadcasted_iota(jnp.int32, (win, LANE), 0)
    for u in range(nsub):
        s0a = pl.multiple_of((s0_ref[base + u] >> 3) << 3, 8)
        lanes = slice(u * LANE, (u + 1) * LANE)
        hit = (rows + s0a) == slot_ref[0, :, lanes]
        comp = _dot(jnp.where(hit, 1.0, 0.0).astype(BF16), h_ref[0, lanes, :])
        xs_scr[pl.ds(s0a, win), :] += comp
        gate_scr[pl.ds(s0a, win), :] += jnp.sum(jnp.where(hit, aff_ref[0, :, lanes], 0.0), axis=1, keepdims=True)

    @pl.when(j == nb - 1)
    def _():
        rc = min(cap, 256)
        for c0 in range(0, cap, rc):
            xs = xs_scr[c0:c0 + rc, :].astype(BF16)
            gg = _dot(xs, wg_ref[0])
            hid = (gg * (1.0 / (1.0 + jnp.exp(-gg)))) * _dot(xs, wu_ref[0])
            out = _dot(hid.astype(BF16), wd_ref[0]) * gate_scr[c0:c0 + rc, :]
            hi, lo = _split2(out)
            ohi_ref[0, 0, c0:c0 + rc, :] = hi
            olo_ref[0, 0, c0:c0 + rc, :] = lo


def _expert_call(h2, slot, aff, s0, wg, wu, wd):
    b, l, d = h2.shape
    nc = l // LANE
    cap = EC_CAPACITY_FACTOR * l // N_EXPERTS
    tb = min(l, 512)
    nsub = tb // LANE
    nb = l // tb
    win = LANE + 8
    f = wg.shape[2]
    row3 = lambda e, bi, j, s: ((bi * N_EXPERTS + e) * nb + j, 0, 0)
    grid_spec = pltpu.PrefetchScalarGridSpec(
        num_scalar_prefetch=1,
        grid=(N_EXPERTS, b, nb),
        in_specs=[pl.BlockSpec((1, tb, d), lambda e, bi, j, s: (bi, j, 0)),
                  pl.BlockSpec((1, 1, tb), row3),
                  pl.BlockSpec((1, 1, tb), row3),
                  pl.BlockSpec((1, d, f), lambda e, bi, j, s: (e, 0, 0)),
                  pl.BlockSpec((1, d, f), lambda e, bi, j, s: (e, 0, 0)),
                  pl.BlockSpec((1, f, d), lambda e, bi, j, s: (e, 0, 0))],
        out_specs=[pl.BlockSpec((1, 1, cap, d), lambda e, bi, j, s: (bi, e, 0, 0))] * 2,
        scratch_shapes=[pltpu.VMEM((cap + win, d), F32), pltpu.VMEM((cap + win, 1), F32)],
    )
    return pl.pallas_call(
        functools.partial(_expert_kernel, nb=nb, nsub=nsub, nc=nc, cap=cap, win=win),
        grid_spec=grid_spec,
        out_shape=[jax.ShapeDtypeStruct((b, N_EXPERTS, cap, d), BF16)] * 2,
        compiler_params=_cparams(("arbitrary", "arbitrary", "arbitrary")),
    )(s0, h2, slot.reshape(b * N_EXPERTS * nb, 1, tb), aff.reshape(b * N_EXPERTS * nb, 1, tb), wg, wu, wd)


def _combine_kernel(k0_ref, ha_ref, la_ref, hb_ref, lb_ref, slot_ref, x1_ref, g2_ref, lng_ref, lnb_ref, o_ref,
                    acc_scr, *, nc, wb):
    b, j, e = pl.program_id(0), pl.program_id(1), pl.program_id(2)

    @pl.when(e == 0)
    def _():
        acc_scr[...] = jnp.zeros(acc_scr.shape, F32)

    ka = k0_ref[(b * N_EXPERTS + e) * nc + j]
    rows = lax.broadcasted_iota(jnp.int32, (wb, LANE), 0)
    slot = slot_ref[0]
    pa = jnp.where((rows + ka * wb) == slot, 1.0, 0.0).astype(BF16)
    pb = jnp.where((rows + (ka + 1) * wb) == slot, 1.0, 0.0).astype(BF16)
    acc_scr[...] += (_dot_tn(pa, ha_ref[0, 0]) + _dot_tn(pa, la_ref[0, 0])
                     + _dot_tn(pb, hb_ref[0, 0]) + _dot_tn(pb, lb_ref[0, 0]))

    @pl.when(e == N_EXPERTS - 1)
    def _():
        z = DEEPNORM_ALPHA * x1_ref[0] + g2_ref[0] * acc_scr[...]
        o_ref[0] = _layer_norm(z, lng_ref[...], lnb_ref[...])


def _combine_call(x1, ohi, olo, slot, k0, g2, lng, lnb):
    b, l, d = x1.shape
    nc = l // LANE
    cap = ohi.shape[2]
    wb = min(cap, LANE)
    nwb = cap // wb
    idx = lambda bi, j, e: (bi * N_EXPERTS + e) * nc + j
    blk_a = pl.BlockSpec((1, 1, wb, d), lambda bi, j, e, k0: (bi, e, k0[idx(bi, j, e)], 0))
    blk_b = pl.BlockSpec((1, 1, wb, d),
                         lambda bi, j, e, k0: (bi, e, jnp.minimum(k0[idx(bi, j, e)] + 1, nwb - 1), 0))
    full = lambda arr: pl.BlockSpec(arr.shape, lambda bi, j, e, k0: (0,) * arr.ndim)
    grid_spec = pltpu.PrefetchScalarGridSpec(
        num_scalar_prefetch=1,
        grid=(b, nc, N_EXPERTS),
        in_specs=[blk_a, blk_a, blk_b, blk_b,
                  pl.BlockSpec((1, 1, LANE), lambda bi, j, e, k0: (idx(bi, j, e), 0, 0)),
                  pl.BlockSpec((1, LANE, d), lambda bi, j, e, k0: (bi, j, 0)),
                  pl.BlockSpec((1, 1, d), lambda bi, j, e, k0: (bi, 0, 0)),
                  full(lng), full(lnb)],
        out_specs=pl.BlockSpec((1, LANE, d), lambda bi, j, e, k0: (bi, j, 0)),
        scratch_shapes=[pltpu.VMEM((LANE, d), F32)],
    )
    return pl.pallas_call(
        functools.partial(_combine_kernel, nc=nc, wb=wb),
        grid_spec=grid_spec,
        out_shape=jax.ShapeDtypeStruct((b, l, d), F32),
        compiler_params=_cparams(("arbitrary", "arbitrary", "arbitrary")),
    )(k0, ohi, olo, ohi, olo, slot.reshape(b * N_EXPERTS * nc, 1, LANE), x1, g2, lng, lnb)


def _pad_heads(w, nheads, dh):
    return jnp.pad(w.reshape(w.shape[0], nheads, dh), ((0, 0), (0, 0), (0, HEAD_PAD - dh))).reshape(w.shape[0], -1)


def _swap_halves(w, nheads, dh):
    q = dh // 4
    return jnp.flip(w.reshape(w.shape[0], nheads, 2, 2, q), axis=3).reshape(w.shape[0], nheads * dh)


def _prep_layer(l, w_in, w_gla_gate_f, b_gla_gate_f, w_gla_gate_b, b_gla_gate_b, g_gla_norm, g_mla_qa, w_mla_qb,
                g_mla_kva, w_mla_kvb, g_gqa_q, g_gqa_k, w_out, ln1_g, ln1_b, w_router, ln2_g, ln2_b):
    wi = w_in[l]
    d = wi.shape[0]
    o = IN_OFF
    aq, ak, av = wi[:, o[9]:o[10]], wi[:, o[10]:o[11]], wi[:, o[11]:o[12]]
    mkr = wi[:, o[8]:o[9]]
    zeros = lambda n: jnp.zeros((d, n), F32)
    g_blk = jnp.concatenate([wi[:, o[4]:o[6]], zeros(32), mkr, zeros(32)], axis=1)
    gs_blk = jnp.concatenate([zeros(64), _swap_halves(mkr, 1, MLA_ROPE), zeros(32)], axis=1)
    w_in_r = jnp.concatenate([
        wi[:, 0:o[4]], wi[:, o[6]:o[7]], wi[:, o[7]:o[8]],
        _pad_heads(aq, GQA_HEADS, GQA_DH), _pad_heads(ak, GQA_KV_HEADS, GQA_DH), av, g_blk,
        _pad_heads(_swap_halves(aq, GQA_HEADS, GQA_DH), GQA_HEADS, GQA_DH),
        _pad_heads(_swap_halves(ak, GQA_KV_HEADS, GQA_DH), GQA_KV_HEADS, GQA_DH), gs_blk], axis=1).astype(BF16)
    assert w_in_r.shape[1] == C_END

    r = GLA_GATE_RANK
    wg = jnp.zeros((LANE, 256), F32)
    wg = wg.at[0:r, 0:128].set(w_gla_gate_f[l]).at[r:2 * r, 128:256].set(w_gla_gate_b[l]).astype(BF16)
    bg = jnp.concatenate([b_gla_gate_f[l], b_gla_gate_b[l]])[None, :]

    qb = w_mla_qb[l].reshape(MLA_Q_RANK, MLA_HEADS, MLA_NOPE + MLA_ROPE)
    qb_rope_sw = _swap_halves(qb[:, :, MLA_NOPE:].reshape(MLA_Q_RANK, -1), MLA_HEADS, MLA_ROPE).reshape(
        MLA_Q_RANK, MLA_HEADS, MLA_ROPE)
    qb_sw = jnp.concatenate([jnp.zeros_like(qb[:, :, :MLA_NOPE]), qb_rope_sw], axis=2)
    padq = lambda t: jnp.pad(t, ((0, 0), (0, 0), (0, HEAD_PAD - t.shape[2]))).reshape(MLA_Q_RANK, -1)
    wqb = jnp.concatenate([padq(qb), padq(qb_sw)], axis=1).astype(BF16)

    kvb = w_mla_kvb[l].reshape(MLA_KV_RANK, MLA_HEADS, MLA_NOPE + MLA_DV)
    kpart = jnp.pad(kvb[:, :, :MLA_NOPE], ((0, 0), (0, 0), (0, HEAD_PAD - MLA_NOPE))).reshape(MLA_KV_RANK, -1)
    wkvb = jnp.concatenate([kpart, kvb[:, :, MLA_NOPE:].reshape(MLA_KV_RANK, -1)], axis=1).astype(BF16)

    def gain_rows(gv, nheads):
        g2 = jnp.tile(gv[None, :], (1, nheads))
        plain = _pad_heads(g2, nheads, GQA_DH)
        swapped = _pad_heads(_swap_halves(g2, nheads, GQA_DH), nheads, GQA_DH)
        return [jnp.pad(t, ((0, 0), (0, 768 - t.shape[1]))) for t in (plain, swapped)]

    gains = jnp.concatenate(gain_rows(g_gqa_q[l], GQA_HEADS) + gain_rows(g_gqa_k[l], GQA_KV_HEADS)
                            + [jnp.zeros((4, 768), F32)], axis=0)
    ii = np.arange(768)
    bsum = jnp.asarray((ii[:, None] // HEAD_PAD) == (ii[None, :] // HEAD_PAD), BF16)
    jj = np.arange(GLA_WIDTH)
    bsum64 = jnp.asarray((jj[:, None] // GLA_DV) == (jj[None, :] // GLA_DV), BF16)
    wr = w_router[l].T
    wr_hi = wr.astype(BF16)
    wr_lo = (wr - wr_hi.astype(F32)).astype(BF16)
    return dict(w_in=w_in_r, wg=wg, bg=bg, g_qa=g_mla_qa[l][None, :], wqb=wqb, g_kva=g_mla_kva[l][None, :],
                wkvb=wkvb, gains=gains, bsum=bsum, g_gla=jnp.tile(g_gla_norm[l][None, :], (1, GLA_HEADS)),
                bsum64=bsum64, w_out=w_out[l].astype(BF16), ln1_g=ln1_g[l][None, :], ln1_b=ln1_b[l][None, :],
                wr_hi=wr_hi, wr_lo=wr_lo, ln2_g=ln2_g[l][None, :], ln2_b=ln2_b[l][None, :])


def _tables(l, with_pos):
    sm = MLA_SCALE * LOG2E
    sg = GQA_SCALE * LOG2E
    z = lambda n: jnp.zeros((l, n), F32)
    one = lambda n: jnp.ones((l, n), F32)
    if with_pos:
        t = jnp.arange(l, dtype=jnp.int32)
        row = (t // GRID_W).astype(F32)[:, None]
        col = (t % GRID_W).astype(F32)[:, None]

        def cs(nfreq):
            inv = ROPE_THETA ** (-jnp.arange(nfreq, dtype=F32) / nfreq)
            ar, ac = row * inv, col * inv
            c = jnp.concatenate([jnp.cos(ar), jnp.cos(ar), jnp.cos(ac), jnp.cos(ac)], axis=1)
            s = jnp.concatenate([-jnp.sin(ar), jnp.sin(ar), -jnp.sin(ac), jnp.sin(ac)], axis=1)
            return c, s

        c32, s32 = cs(MLA_ROPE // 4)
        c64, s64 = cs(GQA_DH // 4)
    else:
        c32, s32, c64, s64 = one(32), z(32), one(64), z(64)
    cm = jnp.concatenate([one(64), c32, z(32)], axis=1) * sm
    smm = jnp.concatenate([z(64), s32, z(32)], axis=1) * sm
    ck = jnp.concatenate([z(64), c32, z(32)], axis=1)
    sk = jnp.concatenate([z(64), s32, z(32)], axis=1)
    cg = jnp.concatenate([c64, z(64)], axis=1)
    sgg = jnp.concatenate([s64, z(64)], axis=1)
    return jnp.concatenate([cm, smm, ck, sk, cg * sg, sgg * sg, cg, sgg], axis=1)


def _heads_t(a, nheads, dh):
    b, l, _ = a.shape
    return a.reshape(b, l, nheads, dh).transpose(0, 2, 3, 1)


def _heads(a, nheads, dh):
    b, l, _ = a.shape
    return a.reshape(b, l, nheads, dh).transpose(0, 2, 1, 3)


def _attend(q, k, v):
    b, lq, _ = q.shape
    ot = _attn_call(_heads_t(q, N_QHEADS, HEAD_PAD), _heads(k, N_KVHEADS, HEAD_PAD), _heads_t(v, N_KVHEADS, ATT_DV))
    return ot.transpose(0, 3, 1, 2).reshape(b, lq, N_QHEADS * ATT_DV)


def _moe(h2, aff, x1, g2, w, wexp):
    b, l, _ = h2.shape
    nc = l // LANE
    cap = EC_CAPACITY_FACTOR * l // N_EXPERTS
    slot, off = _topk_call(aff)
    s0 = off[:, :, 0].reshape(-1)
    ohi, olo = _expert_call(h2, slot, aff, s0, *wexp)
    k0 = s0 // min(cap, LANE)
    return _combine_call(x1, ohi, olo, slot, k0, g2, w['ln2_g'], w['ln2_b'])


def kernel(x, c, ctx, c_ctx, w_mod, b_mod, w_in, w_gla_gate_f, b_gla_gate_f, w_gla_gate_b, b_gla_gate_b, g_gla_norm, g_mla_qa, w_mla_qb, g_mla_kva, w_mla_kvb, g_gqa_q, g_gqa_k, w_out, ln1_g, ln1_b, w_router, w_exp_gate, w_exp_up, w_exp_down, ln2_g, ln2_b):
    b, l, d = x.shape
    lc = ctx.shape[1]
    assert b < 8 and l % 256 == 0 and lc % 256 == 0
    cvec = jnp.zeros((8, d), F32).at[0:b].set(c).at[b].set(c_ctx)
    mods = _mod_call(cvec, w_mod, b_mod)
    tab_l = _tables(l, True)
    tab_c = _tables(lc, False)
    zero_state = jnp.zeros((b, GLA_WIDTH, 128), F32)
    cx = ctx
    for li in range(DEPTH):
        need_ctx = li < DEPTH - 1
        w = _prep_layer(li, w_in, w_gla_gate_f, b_gla_gate_f, w_gla_gate_b, b_gla_gate_b, g_gla_norm, g_mla_qa,
                        w_mla_qb, g_mla_kva, w_mla_kvb, g_gqa_q, g_gqa_k, w_out, ln1_g, ln1_b, w_router, ln2_g,
                        ln2_b)
        wexp = (w_exp_gate[li].astype(BF16), w_exp_up[li].astype(BF16), w_exp_down[li].astype(BF16))
        sh1, sc1, g1, sh2, sc2, g2 = [m[:, None, :] for m in jnp.split(mods[li, 0:b], 6, axis=-1)]
        csh1, csc1, cg1, csh2, csc2, cg2 = [jnp.broadcast_to(m[:, None, :], (b, 1, d))
                                            for m in jnp.split(mods[li, b:b + 1], 6, axis=-1)]
        names = ('gq', 'gk', 'la', 'gv', 'gr', 'q', 'k', 'v')
        pl_ = dict(zip(names, _proj_call(x, sc1, sh1, tab_l, w)))
        pc_ = dict(zip(names, _proj_call(cx, csc1, csh1, tab_c, w)))
        ocf, s_f = _gla_call(pc_, zero_state, 0)
        ocb, s_b = _gla_call(pc_, zero_state, 1)
        olf, _ = _gla_call(pl_, s_f, 0)
        olb, _ = _gla_call(pl_, s_b, 1)
        oatt = _attend(pl_['q'], jnp.concatenate([pc_['k'], pl_['k']], axis=1),
                       jnp.concatenate([pc_['v'], pl_['v']], axis=1))
        x1, h2, aff = _merge_call(x, olf, olb, pl_['gr'], oatt, g1, sc2, sh2, w)
        x = _moe(h2, aff, x1, g2, w, wexp)
        if need_ctx:
            oatt_c = _attend(pc_['q'], pc_['k'], pc_['v'])
            c1, ch2, caff = _merge_call(cx, ocf, ocb, pc_['gr'], oatt_c, cg1, csc2, csh2, w)
            cx = _moe(ch2, caff, c1, cg2, w, wexp)
    return x
```

```python
import functools
import math

import numpy as np
import jax
import jax.numpy as jnp
from jax import lax
from jax.experimental import pallas as pl
from jax.experimental.pallas import tpu as pltpu

F32 = jnp.float32
BF16 = jnp.bfloat16

D_MODEL = 1024
DEPTH = 4
GRID_W = 64
ROPE_THETA = 10000.0
LN_EPS = 1e-6
RMS_EPS = 1e-6
DEEPNORM_ALPHA = (2 * DEPTH) ** 0.25

GLA_HEADS, GLA_DK, GLA_DV, GLA_GATE_RANK, GLA_TAU, GLA_CHUNK = 4, 32, 64, 16, 16.0, 64
MLA_HEADS, MLA_Q_RANK, MLA_KV_RANK, MLA_NOPE, MLA_ROPE, MLA_DV = 6, 384, 256, 64, 32, 64
MLA_SCALE = (MLA_NOPE + MLA_ROPE) ** -0.5
GQA_HEADS, GQA_KV_HEADS, GQA_DH = 6, 2, 64
GQA_GROUP = GQA_HEADS // GQA_KV_HEADS
GQA_SCALE = GQA_DH ** -0.5
N_EXPERTS = 16
EC_CAPACITY_FACTOR = 2
D_EXPERT = 1024

GLA_WIDTH = GLA_HEADS * GLA_DV
IN_SPLITS = (GLA_HEADS * GLA_DK, GLA_HEADS * GLA_DK, GLA_WIDTH, GLA_WIDTH, GLA_GATE_RANK, GLA_GATE_RANK,
             MLA_Q_RANK, MLA_KV_RANK, MLA_ROPE,
             GQA_HEADS * GQA_DH, GQA_KV_HEADS * GQA_DH, GQA_KV_HEADS * GQA_DH)
IN_OFF = [0] + [int(o) for o in np.cumsum(IN_SPLITS)]

LANE = 128
HEAD_PAD = 128
N_QHEADS = MLA_HEADS + GQA_HEADS
N_KVHEADS = MLA_HEADS + GQA_KV_HEADS
ATT_DV = 64
ATT_SUB = 256
ATT_AHEAD = 2
LOG2E = math.log2(math.e)
VMEM_LIMIT = 56 * 1024 * 1024

C_A, C_MQA, C_MKVA, C_AQ, C_AK, C_AV, C_G, C_AQS, C_AKS, C_GS, C_END = (
    0, 768, 1152, 1408, 2176, 2432, 2560, 2688, 3456, 3712, 3840)


def _cparams(sem):
    return pltpu.CompilerParams(dimension_semantics=sem, vmem_limit_bytes=VMEM_LIMIT)


def _dot(a, b):
    return jnp.dot(a, b, preferred_element_type=F32)


def _dot_nt(a, b):
    return lax.dot_general(a, b, (((1,), (1,)), ((), ())), preferred_element_type=F32)


def _dot_tn(a, b):
    return lax.dot_general(a, b, (((0,), (0,)), ((), ())), preferred_element_type=F32)


def _split2(x):
    hi = x.astype(BF16)
    lo = (x - hi.astype(F32)).astype(BF16)
    return hi, lo


def _split3(x):
    hi = x.astype(BF16)
    r = x - hi.astype(F32)
    mid = r.astype(BF16)
    lo = (r - mid.astype(F32)).astype(BF16)
    return hi, mid, lo


def _layer_norm(z, g, b):
    mu = jnp.mean(z, axis=-1, keepdims=True)
    zc = z - mu
    var = jnp.mean(zc * zc, axis=-1, keepdims=True)
    return zc * lax.rsqrt(var + LN_EPS) * g + b


def _mod_kernel(c_ref, w_ref, b_ref, o_ref):
    cv = c_ref[...]
    s = cv * (1.0 / (1.0 + jnp.exp(-cv)))
    o_ref[0] = _dot(s.astype(BF16), w_ref[0].astype(BF16)) + b_ref[0]


def _mod_call(cvec, w_mod, b_mod):
    depth, d, n6 = w_mod.shape
    bn = 768
    return pl.pallas_call(
        _mod_kernel,
        grid=(depth, n6 // bn),
        in_specs=[pl.BlockSpec((8, d), lambda l, j: (0, 0)),
                  pl.BlockSpec((1, d, bn), lambda l, j: (l, 0, j)),
                  pl.BlockSpec((1, 1, bn), lambda l, j: (l, 0, j))],
        out_specs=pl.BlockSpec((1, 8, bn), lambda l, j: (l, 0, j)),
        out_shape=jax.ShapeDtypeStruct((depth, 8, n6), F32),
        compiler_params=_cparams(("arbitrary", "arbitrary")),
    )(cvec, w_mod, b_mod.reshape(depth, 1, n6))


def _proj_kernel(x_ref, sc_ref, sh_ref, tab_ref, w_ref, wg_ref, bg_ref, gqa_ref, wqb_ref, gkva_ref, wkvb_ref,
                 gains_ref, bsum_ref,
                 gq_ref, gk_ref, la_ref, gv_ref, gr_ref, q_ref, k_ref, v_ref):
    h = x_ref[0] * (1.0 + sc_ref[0]) + sh_ref[0]
    hb = h.astype(BF16)

    def mm(a, b):
        return _dot(hb, w_ref[:, a:b])

    def tab(i):
        return tab_ref[:, i * LANE:(i + 1) * LANE]

    a = mm(C_A, C_MQA)
    gq_ref[0] = a[:, 0:128] * (GLA_DK ** -0.5)
    gk_ref[0] = a[:, 128:256]
    gv_ref[0] = a[:, 256:512]
    gr_ref[0] = a[:, 512:768]

    g = mm(C_G, C_AQS)
    gs = mm(C_GS, C_END)
    lg = _dot(g.astype(BF16), wg_ref[...]) + bg_ref[...]
    la_ref[0] = (jnp.minimum(lg, 0.0) - jnp.log1p(jnp.exp(-jnp.abs(lg)))) * (1.0 / GLA_TAU)

    mqa = mm(C_MQA, C_MKVA)
    nq = mqa * lax.rsqrt(jnp.mean(mqa * mqa, axis=-1, keepdims=True) + RMS_EPS) * gqa_ref[...]
    q2 = _dot(nq.astype(BF16), wqb_ref[...])
    cm, sm = tab(0), tab(1)
    for hh in range(MLA_HEADS):
        lo = hh * HEAD_PAD
        qh = q2[:, lo:lo + HEAD_PAD] * cm + q2[:, 768 + lo:768 + lo + HEAD_PAD] * sm
        q_ref[0, :, lo:lo + HEAD_PAD] = qh.astype(BF16)

    mkva = mm(C_MKVA, C_AQ)
    nkv = mkva * lax.rsqrt(jnp.mean(mkva * mkva, axis=-1, keepdims=True) + RMS_EPS) * gkva_ref[...]
    kv = _dot(nkv.astype(BF16), wkvb_ref[...])
    kr = g * tab(2) + gs * tab(3)
    for hh in range(MLA_HEADS):
        lo = hh * HEAD_PAD
        k_ref[0, :, lo:lo + HEAD_PAD] = (kv[:, lo:lo + HEAD_PAD] + kr).astype(BF16)
    v_ref[0, :, 0:384] = kv[:, 768:1152].astype(BF16)

    def head_rs(t, width):
        hi, lo = _split2(t * t)
        bs = bsum_ref[0:width, 0:width]
        ms = (_dot(hi, bs) + _dot(lo, bs)) * (1.0 / GQA_DH)
        return lax.rsqrt(ms + RMS_EPS)

    aq = mm(C_AQ, C_AK)
    aqs = mm(C_AQS, C_AKS)
    rs = head_rs(aq, 768)
    gq_g, gq_s = gains_ref[0:1, 0:768], gains_ref[1:2, 0:768]
    cg, sg = tab(4), tab(5)
    for hh in range(GQA_HEADS):
        lo = hh * HEAD_PAD
        sl = slice(lo, lo + HEAD_PAD)
        qh = aq[:, sl] * rs[:, sl] * gq_g[:, sl] * cg + aqs[:, sl] * rs[:, sl] * gq_s[:, sl] * sg
        q_ref[0, :, 768 + lo:768 + lo + HEAD_PAD] = qh.astype(BF16)

    ak = mm(C_AK, C_AV)
    aks = mm(C_AKS, C_GS)
    rsk = head_rs(ak, 256)
    gk_g, gk_s = gains_ref[2:3, 0:256], gains_ref[3:4, 0:256]
    ck, sk = tab(6), tab(7)
    for hh in range(GQA_KV_HEADS):
        lo = hh * HEAD_PAD
        sl = slice(lo, lo + HEAD_PAD)
        kh = ak[:, sl] * rsk[:, sl] * gk_g[:, sl] * ck + aks[:, sl] * rsk[:, sl] * gk_s[:, sl] * sk
        k_ref[0, :, 768 + lo:768 + lo + HEAD_PAD] = kh.astype(BF16)
    v_ref[0, :, 384:512] = mm(C_AV, C_G).astype(BF16)


def _proj_call(x, sc, sh, tab, w):
    b, l, d = x.shape
    t = min(l, 256)
    full = lambda arr: pl.BlockSpec(arr.shape, lambda bi, i: (0,) * arr.ndim)
    row = lambda n: pl.BlockSpec((1, t, n), lambda bi, i: (bi, i, 0))
    vec = pl.BlockSpec((1, 1, d), lambda bi, i: (bi, 0, 0))
    consts = [w['w_in'], w['wg'], w['bg'], w['g_qa'], w['wqb'], w['g_kva'], w['wkvb'], w['gains'], w['bsum']]
    outs = [(128, F32), (128, F32), (256, F32), (256, F32), (256, F32), (1536, BF16), (1024, BF16), (512, BF16)]
    return pl.pallas_call(
        _proj_kernel,
        grid=(b, l // t),
        in_specs=[row(d), vec, vec, pl.BlockSpec((t, 8 * LANE), lambda bi, i: (i, 0))] + [full(a) for a in consts],
        out_specs=[row(n) for n, _ in outs],
        out_shape=[jax.ShapeDtypeStruct((b, l, n), dt) for n, dt in outs],
        compiler_params=_cparams(("arbitrary", "arbitrary")),
    )(x, sc, sh, tab, *consts)


GLA_BLOCK = 256


def _gla_kernel(q_ref, k_ref, la_ref, v_ref, s0_ref, o_ref, sfin_ref, s_scr, *, reverse, nblk):
    i = pl.program_id(1)

    @pl.when(i == 0)
    def _():
        s_scr[...] = s0_ref[0]

    r = q_ref.shape[1]
    nch = r // GLA_CHUNK
    la = la_ref[0]
    q = q_ref[0]
    k = k_ref[0]
    v = v_ref[0]
    ri = lax.broadcasted_iota(jnp.int32, (r, r), 0)
    ci = lax.broadcasted_iota(jnp.int32, (r, r), 1)
    same = (ri >> 6) == (ci >> 6)
    tri = jnp.logical_and(same, (ci >= ri) if reverse else (ci <= ri))
    trib = jnp.where(tri, 1.0, 0.0).astype(BF16)
    oneb = jnp.where(same, 1.0, 0.0).astype(BF16)
    hi, mid, lo = _split3(la)
    bcum = _dot(trib, hi) + _dot(trib, mid) + _dot(trib, lo)
    tot = _dot(oneb, hi) + _dot(oneb, mid) + _dot(oneb, lo)
    qd = q * jnp.exp(bcum)
    kinv = (k * jnp.exp(-bcum)).astype(BF16)
    kend = (k * jnp.exp(tot - bcum)).astype(BF16)
    dec = jnp.exp(tot)
    vb = v.astype(BF16)
    lane_k = lax.broadcasted_iota(jnp.int32, (1, GLA_HEADS * GLA_DK), 1)
    lane_v = lax.broadcasted_iota(jnp.int32, (1, GLA_WIDTH), 1)
    o = jnp.zeros((r, GLA_WIDTH), F32)
    for hh in range(GLA_HEADS):
        qh = jnp.where((lane_k >> 5) == hh, qd, 0.0).astype(BF16)
        att = jnp.where(tri, _dot_nt(qh, kinv), 0.0)
        vh = jnp.where((lane_v >> 6) == hh, v, 0.0).astype(BF16)
        o = o + _dot(att.astype(BF16), vh)
    qdb = qd.astype(BF16)
    bd = (lax.broadcasted_iota(jnp.int32, (GLA_WIDTH, GLA_HEADS * GLA_DK), 0) >> 6) == (
        lax.broadcasted_iota(jnp.int32, (GLA_WIDTH, GLA_HEADS * GLA_DK), 1) >> 5)
    s = s_scr[...]
    order = range(nch - 1, -1, -1) if reverse else range(nch)
    for cc in order:
        rows = slice(cc * GLA_CHUNK, (cc + 1) * GLA_CHUNK)
        o_ref[0, rows, :] = o[rows] + _dot_nt(qdb[rows], s.astype(BF16))
        ut = _dot_tn(vb[rows], kend[rows])
        s = dec[cc * GLA_CHUNK:cc * GLA_CHUNK + 1, :] * s + jnp.where(bd, ut, 0.0)
    s_scr[...] = s

    @pl.when(i == nblk - 1)
    def _():
        sfin_ref[0] = s


def _gla_call(p, s0, direction):
    b, l, _ = p['gq'].shape
    r = min(l, GLA_BLOCK)
    nblk = l // r
    reverse = direction == 1
    blk = (lambda i: nblk - 1 - i) if reverse else (lambda i: i)
    row = lambda n, col=0: pl.BlockSpec((1, r, n), lambda bi, i: (bi, blk(i), col))
    st = pl.BlockSpec((1, GLA_WIDTH, 128), lambda bi, i: (bi, 0, 0))
    return pl.pallas_call(
        functools.partial(_gla_kernel, reverse=reverse, nblk=nblk),
        grid=(b, nblk),
        in_specs=[row(128), row(128), row(128, direction), row(256), st],
        out_specs=[row(256), st],
        out_shape=[jax.ShapeDtypeStruct((b, l, GLA_WIDTH), F32), jax.ShapeDtypeStruct((b, GLA_WIDTH, 128), F32)],
        scratch_shapes=[pltpu.VMEM((GLA_WIDTH, 128), F32)],
        compiler_params=_cparams(("arbitrary", "arbitrary")),
    )(p['gq'], p['gk'], p['la'], p['gv'], s0)


def _attn_kernel(qt_ref, k_ref, vt_ref, o_ref, m_scr, l_scr, acc_scr, *, nk):
    ki = pl.program_id(3)

    @pl.when(ki == 0)
    def _():
        m_scr[...] = jnp.full(m_scr.shape, -jnp.inf, F32)
        l_scr[...] = jnp.zeros(l_scr.shape, F32)
        acc_scr[...] = jnp.zeros(acc_scr.shape, F32)

    qt = qt_ref[0, 0]
    tq = qt.shape[1]
    m, l8, acc = m_scr[...], l_scr[...], acc_scr[...]
    nsub = k_ref.shape[2] // ATT_SUB

    def scores(c):
        return _dot(k_ref[0, 0, c * ATT_SUB:(c + 1) * ATT_SUB, :], qt)

    def weighted_values(acc_in, alpha_c, p_c, c):
        return alpha_c * acc_in + _dot(vt_ref[0, 0, :, c * ATT_SUB:(c + 1) * ATT_SUB], p_c)

    ahead = [scores(c) for c in range(min(ATT_AHEAD, nsub))]
    pending = None
    for c in range(nsub):
        s = ahead.pop(0)
        if c + ATT_AHEAD < nsub:
            ahead.append(scores(c + ATT_AHEAD))
        m_new = jnp.maximum(m, jnp.max(s, axis=0, keepdims=True))
        alpha = jnp.exp2(m - m_new)
        p = jnp.exp2(s - m_new)
        l8 = alpha * l8 + jnp.sum(p.reshape(ATT_SUB // 8, 8, tq), axis=0)
        m = m_new
        if pending is not None:
            acc = weighted_values(acc, *pending)
        pending = (alpha, p.astype(BF16), c)
    acc = weighted_values(acc, *pending)
    m_scr[...], l_scr[...], acc_scr[...] = m, l8, acc

    @pl.when(ki == nk - 1)
    def _():
        o_ref[0, 0] = (acc_scr[...] / jnp.sum(l_scr[...], axis=0, keepdims=True)).astype(o_ref.dtype)


def _pick_tile(n, options):
    for t in options:
        if n % t == 0:
            return t
    raise ValueError(f"no tile for {n}")


def _attn_call(qt, k, vt):
    b, nh, dk, lq = qt.shape
    lk = k.shape[2]
    tq = _pick_tile(lq, (1024, 512, 256))
    tk = _pick_tile(lk, (2816, 768, 512, 256))
    nk = lk // tk

    def kvh(h):
        return jnp.where(h < MLA_HEADS, h, MLA_HEADS + (h - MLA_HEADS) // GQA_GROUP)

    return pl.pallas_call(
        functools.partial(_attn_kernel, nk=nk),
        grid=(b, nh, lq // tq, nk),
        in_specs=[pl.BlockSpec((1, 1, dk, tq), lambda bi, h, qi, ki: (bi, h, 0, qi)),
                  pl.BlockSpec((1, 1, tk, dk), lambda bi, h, qi, ki: (bi, kvh(h), ki, 0)),
                  pl.BlockSpec((1, 1, ATT_DV, tk), lambda bi, h, qi, ki: (bi, kvh(h), 0, ki))],
        out_specs=pl.BlockSpec((1, 1, ATT_DV, tq), lambda bi, h, qi, ki: (bi, h, 0, qi)),
        out_shape=jax.ShapeDtypeStruct((b, nh, ATT_DV, lq), BF16),
        scratch_shapes=[pltpu.VMEM((1, tq), F32), pltpu.VMEM((8, tq), F32), pltpu.VMEM((ATT_DV, tq), F32)],
        compiler_params=_cparams(("arbitrary", "arbitrary", "arbitrary", "arbitrary")),
    )(qt, k, vt)


def _merge_kernel(x_ref, of_ref, ob_ref, gr_ref, oa_ref, g1_ref, sc_ref, sh_ref, ggla_ref, bs_ref, wout_ref,
                  lng_ref, lnb_ref, wrh_ref, wrl_ref, x1_ref, h2_ref, aff_ref):
    o = of_ref[0] + ob_ref[0]
    hi, lo = _split2(o * o)
    ms = (_dot(hi, bs_ref[...]) + _dot(lo, bs_ref[...])) * (1.0 / GLA_DV)
    r = gr_ref[0]
    gla = o * lax.rsqrt(ms + RMS_EPS) * ggla_ref[...] * (r * (1.0 / (1.0 + jnp.exp(-r))))
    a = _dot(gla.astype(BF16), wout_ref[0:GLA_WIDTH, :]) + _dot(oa_ref[0], wout_ref[GLA_WIDTH:, :])
    x1 = _layer_norm(DEEPNORM_ALPHA * x_ref[0] + g1_ref[0] * a, lng_ref[...], lnb_ref[...])
    x1_ref[0] = x1
    h2 = x1 * (1.0 + sc_ref[0]) + sh_ref[0]
    h2h, h2l = _split2(h2)
    h2_ref[0] = h2h
    wh, wl = wrh_ref[...], wrl_ref[...]
    logits = _dot_nt(wh, h2h) + _dot_nt(wh, h2l) + _dot_nt(wl, h2h)
    e = jnp.exp(logits - jnp.max(logits, axis=0, keepdims=True))
    aff_ref[0] = e / jnp.sum(e, axis=0, keepdims=True)


def _merge_call(x, of, ob, gr, oatt, g1, sc2, sh2, w):
    b, l, d = x.shape
    t = min(l, 256)
    full = lambda arr: pl.BlockSpec(arr.shape, lambda bi, i: (0,) * arr.ndim)
    row = lambda n: pl.BlockSpec((1, t, n), lambda bi, i: (bi, i, 0))
    vec = pl.BlockSpec((1, 1, d), lambda bi, i: (bi, 0, 0))
    consts = [w['g_gla'], w['bsum64'], w['w_out'], w['ln1_g'], w['ln1_b'], w['wr_hi'], w['wr_lo']]
    return pl.pallas_call(
        _merge_kernel,
        grid=(b, l // t),
        in_specs=[row(d), row(256), row(256), row(256), row(768), vec, vec, vec] + [full(a) for a in consts],
        out_specs=[row(d), row(d), pl.BlockSpec((1, N_EXPERTS, t), lambda bi, i: (bi, 0, i))],
        out_shape=[jax.ShapeDtypeStruct((b, l, d), F32), jax.ShapeDtypeStruct((b, l, d), BF16),
                   jax.ShapeDtypeStruct((b, N_EXPERTS, l), F32)],
        compiler_params=_cparams(("arbitrary", "arbitrary")),
    )(x, of, ob, gr, oatt, g1, sc2, sh2, *consts)


def _topk_kernel(aff_ref, slot_ref, off_ref, *, nc, cap):
    shift = int(math.log2(nc))
    bits = pltpu.bitcast(aff_ref[0], jnp.int32)
    r = bits.shape[0]
    ri = lax.broadcasted_iota(jnp.int32, (r, r), 0)
    ci = lax.broadcasted_iota(jnp.int32, (r, r), 1)
    same = (ri >> shift) == (ci >> shift)
    ones_bd = jnp.where(same, 1.0, 0.0).astype(BF16)
    lstrict = jnp.where(jnp.logical_and(same, ci < ri), 1.0, 0.0).astype(BF16)
    li = lax.broadcasted_iota(jnp.int32, (LANE, LANE), 0)
    lj = lax.broadcasted_iota(jnp.int32, (LANE, LANE), 1)
    uex = jnp.where(li < lj, 1.0, 0.0).astype(BF16)
    ones = jnp.ones((LANE, LANE), BF16)

    def total(mask):
        colsum = _dot(ones_bd, jnp.where(mask, 1.0, 0.0).astype(BF16))
        return _dot(colsum.astype(BF16), ones)

    def excl_cumsum(mask):
        mb = jnp.where(mask, 1.0, 0.0).astype(BF16)
        off = _dot(lstrict, _dot(mb, ones).astype(BF16))
        return _dot(mb, uex) + off, off

    def body(it, v):
        cand = v | lax.shift_left(jnp.int32(1), jnp.int32(30) - it)
        return jnp.where(total(bits >= cand) >= cap, cand, v)

    v = lax.fori_loop(0, 31, body, jnp.zeros(bits.shape, jnp.int32))
    gt = bits > v
    eq = bits == v
    need = cap - total(gt)
    rank_eq, _ = excl_cumsum(eq)
    sel = jnp.logical_or(gt, jnp.logical_and(eq, rank_eq < need))
    pos, off = excl_cumsum(sel)
    slot_ref[0] = jnp.where(sel, pos.astype(jnp.int32), -1)
    off_ref[0] = off.astype(jnp.int32)


def _topk_call(aff):
    b, e, l = aff.shape
    nc = l // LANE
    cap = EC_CAPACITY_FACTOR * l // N_EXPERTS
    r = e * nc
    spec = pl.BlockSpec((1, r, LANE), lambda bi: (bi, 0, 0))
    return pl.pallas_call(
        functools.partial(_topk_kernel, nc=nc, cap=cap),
        grid=(b,),
        in_specs=[spec],
        out_specs=[spec, spec],
        out_shape=[jax.ShapeDtypeStruct((b, r, LANE), jnp.int32)] * 2,
        compiler_params=_cparams(("arbitrary",)),
    )(aff.reshape(b, r, LANE))


def _expert_kernel(s0_ref, h_ref, slot_ref, aff_ref, wg_ref, wu_ref, wd_ref, ohi_ref, olo_ref, xs_scr, gate_scr,
                   *, nb, nsub, nc, cap, win):
    e, b, j = pl.program_id(0), pl.program_id(1), pl.program_id(2)

    @pl.when(j == 0)
    def _():
        xs_scr[...] = jnp.zeros(xs_scr.shape, F32)
        gate_scr[...] = jnp.zeros(gate_scr.shape, F32)

    base = (b * N_EXPERTS + e) * nc + j * nsub
    rows = lax.broadcasted_iota(jnp.int32, (win, LANE), 0)
    for u in range(nsub):
        s0a = pl.multiple_of((s0_ref[base + u] >> 3) << 3, 8)
        lanes = slice(u * LANE, (u + 1) * LANE)
        hit = (rows + s0a) == slot_ref[0, :, lanes]
        comp = _dot(jnp.where(hit, 1.0, 0.0).astype(BF16), h_ref[0, lanes, :])
        xs_scr[pl.ds(s0a, win), :] += comp
        gate_scr[pl.ds(s0a, win), :] += jnp.sum(jnp.where(hit, aff_ref[0, :, lanes], 0.0), axis=1, keepdims=True)

    @pl.when(j == nb - 1)
    def _():
        rc = min(cap, 256)
        for c0 in range(0, cap, rc):
            xs = xs_scr[c0:c0 + rc, :].astype(BF16)
            gg = _dot(xs, wg_ref[0])
            hid = (gg * (1.0 / (1.0 + jnp.exp(-gg)))) * _dot(xs, wu_ref[0])
            out = _dot(hid.astype(BF16), wd_ref[0]) * gate_scr[c0:c0 + rc, :]
            hi, lo = _split2(out)
            ohi_ref[0, 0, c0:c0 + rc, :] = hi
            olo_ref[0, 0, c0:c0 + rc, :] = lo
        pad = jnp.zeros((ohi_ref.shape[2] - cap, ohi_ref.shape[3]), BF16)
        ohi_ref[0, 0, cap:, :] = pad
        olo_ref[0, 0, cap:, :] = pad


def _expert_call(h2, slot, aff, s0, wg, wu, wd):
    b, l, d = h2.shape
    nc = l // LANE
    cap = EC_CAPACITY_FACTOR * l // N_EXPERTS
    tb = min(l, 512)
    nsub = tb // LANE
    nb = l // tb
    win = LANE + 8
    f = wg.shape[2]
    row3 = lambda e, bi, j, s: ((bi * N_EXPERTS + e) * nb + j, 0, 0)
    grid_spec = pltpu.PrefetchScalarGridSpec(
        num_scalar_prefetch=1,
        grid=(N_EXPERTS, b, nb),
        in_specs=[pl.BlockSpec((1, tb, d), lambda e, bi, j, s: (bi, j, 0)),
                  pl.BlockSpec((1, 1, tb), row3),
                  pl.BlockSpec((1, 1, tb), row3),
                  pl.BlockSpec((1, d, f), lambda e, bi, j, s: (e, 0, 0)),
                  pl.BlockSpec((1, d, f), lambda e, bi, j, s: (e, 0, 0)),
                  pl.BlockSpec((1, f, d), lambda e, bi, j, s: (e, 0, 0))],
        out_specs=[pl.BlockSpec((1, 1, cap + COMBINE_WIN, d), lambda e, bi, j, s: (bi, e, 0, 0))] * 2,
        scratch_shapes=[pltpu.VMEM((cap + win, d), F32), pltpu.VMEM((cap + win, 1), F32)],
    )
    return pl.pallas_call(
        functools.partial(_expert_kernel, nb=nb, nsub=nsub, nc=nc, cap=cap, win=win),
        grid_spec=grid_spec,
        out_shape=[jax.ShapeDtypeStruct((b, N_EXPERTS, cap + COMBINE_WIN, d), BF16)] * 2,
        compiler_params=_cparams(("arbitrary", "arbitrary", "arbitrary")),
    )(s0, h2, slot.reshape(b * N_EXPERTS * nb, 1, tb), aff.reshape(b * N_EXPERTS * nb, 1, tb), wg, wu, wd)


COMBINE_WIN = LANE + 16


def _combine_kernel(s0_ref, ohi_hbm, olo_hbm, slot_ref, x1_ref, g2_ref, lng_ref, lnb_ref, o_ref,
                    hbuf, lbuf, sem, *, nc, nsteps):
    b, j = pl.program_id(0), pl.program_id(1)
    t = b * nc + j
    cur = t % 2

    def win_start(bb, jj, e):
        return pl.multiple_of((s0_ref[(bb * N_EXPERTS + e) * nc + jj] >> 4) << 4, 16)

    def copies(bb, jj, sl):
        out = []
        for e in range(N_EXPERTS):
            rows = pl.ds(win_start(bb, jj, e), COMBINE_WIN)
            dst = pl.ds(e * COMBINE_WIN, COMBINE_WIN)
            out.append(pltpu.make_async_copy(ohi_hbm.at[bb, e, rows, :], hbuf.at[sl, dst, :], sem.at[sl, 0]))
            out.append(pltpu.make_async_copy(olo_hbm.at[bb, e, rows, :], lbuf.at[sl, dst, :], sem.at[sl, 1]))
        return out

    @pl.when(t == 0)
    def _():
        for cp in copies(b, j, cur):
            cp.start()

    @pl.when(t + 1 < nsteps)
    def _():
        wrap = j + 1 == nc
        for cp in copies(jnp.where(wrap, b + 1, b), jnp.where(wrap, 0, j + 1), 1 - cur):
            cp.start()

    for cp in copies(b, j, cur):
        cp.wait()

    rows = lax.broadcasted_iota(jnp.int32, (COMBINE_WIN, LANE), 0)
    onehot = [jnp.where((rows + win_start(b, j, e)) == slot_ref[0, e:e + 1, :], 1.0, 0.0).astype(BF16)
              for e in range(N_EXPERTS)]
    p = jnp.concatenate(onehot, axis=0)
    acc = _dot_tn(p, hbuf[cur]) + _dot_tn(p, lbuf[cur])
    z = DEEPNORM_ALPHA * x1_ref[0] + g2_ref[0] * acc
    o_ref[0] = _layer_norm(z, lng_ref[...], lnb_ref[...])


def _combine_call(x1, ohi, olo, slot, s0, g2, lng, lnb):
    b, l, d = x1.shape
    nc = l // LANE
    full = lambda arr: pl.BlockSpec(arr.shape, lambda bi, j, s: (0,) * arr.ndim)
    grid_spec = pltpu.PrefetchScalarGridSpec(
        num_scalar_prefetch=1,
        grid=(b, nc),
        in_specs=[pl.BlockSpec(memory_space=pl.ANY), pl.BlockSpec(memory_space=pl.ANY),
                  pl.BlockSpec((1, N_EXPERTS, LANE), lambda bi, j, s: (bi, 0, j)),
                  pl.BlockSpec((1, LANE, d), lambda bi, j, s: (bi, j, 0)),
                  pl.BlockSpec((1, 1, d), lambda bi, j, s: (bi, 0, 0)),
                  full(lng), full(lnb)],
        out_specs=pl.BlockSpec((1, LANE, d), lambda bi, j, s: (bi, j, 0)),
        scratch_shapes=[pltpu.VMEM((2, N_EXPERTS * COMBINE_WIN, d), BF16),
                        pltpu.VMEM((2, N_EXPERTS * COMBINE_WIN, d), BF16),
                        pltpu.SemaphoreType.DMA((2, 2))],
    )
    return pl.pallas_call(
        functools.partial(_combine_kernel, nc=nc, nsteps=b * nc),
        grid_spec=grid_spec,
        out_shape=jax.ShapeDtypeStruct((b, l, d), F32),
        compiler_params=_cparams(("arbitrary", "arbitrary")),
    )(s0, ohi, olo, slot.reshape(b, N_EXPERTS, l), x1, g2, lng, lnb)


def _pad_heads(w, nheads, dh):
    return jnp.pad(w.reshape(w.shape[0], nheads, dh), ((0, 0), (0, 0), (0, HEAD_PAD - dh))).reshape(w.shape[0], -1)


def _swap_halves(w, nheads, dh):
    q = dh // 4
    return jnp.flip(w.reshape(w.shape[0], nheads, 2, 2, q), axis=3).reshape(w.shape[0], nheads * dh)


def _prep_layer(l, w_in, w_gla_gate_f, b_gla_gate_f, w_gla_gate_b, b_gla_gate_b, g_gla_norm, g_mla_qa, w_mla_qb,
                g_mla_kva, w_mla_kvb, g_gqa_q, g_gqa_k, w_out, ln1_g, ln1_b, w_router, ln2_g, ln2_b):
    wi = w_in[l]
    d = wi.shape[0]
    o = IN_OFF
    aq, ak, av = wi[:, o[9]:o[10]], wi[:, o[10]:o[11]], wi[:, o[11]:o[12]]
    mkr = wi[:, o[8]:o[9]]
    zeros = lambda n: jnp.zeros((d, n), F32)
    g_blk = jnp.concatenate([wi[:, o[4]:o[6]], zeros(32), mkr, zeros(32)], axis=1)
    gs_blk = jnp.concatenate([zeros(64), _swap_halves(mkr, 1, MLA_ROPE), zeros(32)], axis=1)
    w_in_r = jnp.concatenate([
        wi[:, 0:o[4]], wi[:, o[6]:o[7]], wi[:, o[7]:o[8]],
        _pad_heads(aq, GQA_HEADS, GQA_DH), _pad_heads(ak, GQA_KV_HEADS, GQA_DH), av, g_blk,
        _pad_heads(_swap_halves(aq, GQA_HEADS, GQA_DH), GQA_HEADS, GQA_DH),
        _pad_heads(_swap_halves(ak, GQA_KV_HEADS, GQA_DH), GQA_KV_HEADS, GQA_DH), gs_blk], axis=1).astype(BF16)
    assert w_in_r.shape[1] == C_END

    r = GLA_GATE_RANK
    wg = jnp.zeros((LANE, 256), F32)
    wg = wg.at[0:r, 0:128].set(w_gla_gate_f[l]).at[r:2 * r, 128:256].set(w_gla_gate_b[l]).astype(BF16)
    bg = jnp.concatenate([b_gla_gate_f[l], b_gla_gate_b[l]])[None, :]

    qb = w_mla_qb[l].reshape(MLA_Q_RANK, MLA_HEADS, MLA_NOPE + MLA_ROPE)
    qb_rope_sw = _swap_halves(qb[:, :, MLA_NOPE:].reshape(MLA_Q_RANK, -1), MLA_HEADS, MLA_ROPE).reshape(
        MLA_Q_RANK, MLA_HEADS, MLA_ROPE)
    qb_sw = jnp.concatenate([jnp.zeros_like(qb[:, :, :MLA_NOPE]), qb_rope_sw], axis=2)
    padq = lambda t: jnp.pad(t, ((0, 0), (0, 0), (0, HEAD_PAD - t.shape[2]))).reshape(MLA_Q_RANK, -1)
    wqb = jnp.concatenate([padq(qb), padq(qb_sw)], axis=1).astype(BF16)

    kvb = w_mla_kvb[l].reshape(MLA_KV_RANK, MLA_HEADS, MLA_NOPE + MLA_DV)
    kpart = jnp.pad(kvb[:, :, :MLA_NOPE], ((0, 0), (0, 0), (0, HEAD_PAD - MLA_NOPE))).reshape(MLA_KV_RANK, -1)
    wkvb = jnp.concatenate([kpart, kvb[:, :, MLA_NOPE:].reshape(MLA_KV_RANK, -1)], axis=1).astype(BF16)

    def gain_rows(gv, nheads):
        g2 = jnp.tile(gv[None, :], (1, nheads))
        plain = _pad_heads(g2, nheads, GQA_DH)
        swapped = _pad_heads(_swap_halves(g2, nheads, GQA_DH), nheads, GQA_DH)
        return [jnp.pad(t, ((0, 0), (0, 768 - t.shape[1]))) for t in (plain, swapped)]

    gains = jnp.concatenate(gain_rows(g_gqa_q[l], GQA_HEADS) + gain_rows(g_gqa_k[l], GQA_KV_HEADS)
                            + [jnp.zeros((4, 768), F32)], axis=0)
    ii = np.arange(768)
    bsum = jnp.asarray((ii[:, None] // HEAD_PAD) == (ii[None, :] // HEAD_PAD), BF16)
    jj = np.arange(GLA_WIDTH)
    bsum64 = jnp.asarray((jj[:, None] // GLA_DV) == (jj[None, :] // GLA_DV), BF16)
    wr = w_router[l].T
    wr_hi = wr.astype(BF16)
    wr_lo = (wr - wr_hi.astype(F32)).astype(BF16)
    return dict(w_in=w_in_r, wg=wg, bg=bg, g_qa=g_mla_qa[l][None, :], wqb=wqb, g_kva=g_mla_kva[l][None, :],
                wkvb=wkvb, gains=gains, bsum=bsum, g_gla=jnp.tile(g_gla_norm[l][None, :], (1, GLA_HEADS)),
                bsum64=bsum64, w_out=w_out[l].astype(BF16), ln1_g=ln1_g[l][None, :], ln1_b=ln1_b[l][None, :],
                wr_hi=wr_hi, wr_lo=wr_lo, ln2_g=ln2_g[l][None, :], ln2_b=ln2_b[l][None, :])


def _tables(l, with_pos):
    sm = MLA_SCALE * LOG2E
    sg = GQA_SCALE * LOG2E
    z = lambda n: jnp.zeros((l, n), F32)
    one = lambda n: jnp.ones((l, n), F32)
    if with_pos:
        t = jnp.arange(l, dtype=jnp.int32)
        row = (t // GRID_W).astype(F32)[:, None]
        col = (t % GRID_W).astype(F32)[:, None]

        def cs(nfreq):
            inv = ROPE_THETA ** (-jnp.arange(nfreq, dtype=F32) / nfreq)
            ar, ac = row * inv, col * inv
            c = jnp.concatenate([jnp.cos(ar), jnp.cos(ar), jnp.cos(ac), jnp.cos(ac)], axis=1)
            s = jnp.concatenate([-jnp.sin(ar), jnp.sin(ar), -jnp.sin(ac), jnp.sin(ac)], axis=1)
            return c, s

        c32, s32 = cs(MLA_ROPE // 4)
        c64, s64 = cs(GQA_DH // 4)
    else:
        c32, s32, c64, s64 = one(32), z(32), one(64), z(64)
    cm = jnp.concatenate([one(64), c32, z(32)], axis=1) * sm
    smm = jnp.concatenate([z(64), s32, z(32)], axis=1) * sm
    ck = jnp.concatenate([z(64), c32, z(32)], axis=1)
    sk = jnp.concatenate([z(64), s32, z(32)], axis=1)
    cg = jnp.concatenate([c64, z(64)], axis=1)
    sgg = jnp.concatenate([s64, z(64)], axis=1)
    return jnp.concatenate([cm, smm, ck, sk, cg * sg, sgg * sg, cg, sgg], axis=1)


def _heads_t(a, nheads, dh):
    b, l, _ = a.shape
    return a.reshape(b, l, nheads, dh).transpose(0, 2, 3, 1)


def _heads(a, nheads, dh):
    b, l, _ = a.shape
    return a.reshape(b, l, nheads, dh).transpose(0, 2, 1, 3)


def _attend(q, k, v):
    b, lq, _ = q.shape
    ot = _attn_call(_heads_t(q, N_QHEADS, HEAD_PAD), _heads(k, N_KVHEADS, HEAD_PAD), _heads_t(v, N_KVHEADS, ATT_DV))
    return ot.transpose(0, 3, 1, 2).reshape(b, lq, N_QHEADS * ATT_DV)


def _moe(h2, aff, x1, g2, w, wexp):
    b, l, _ = h2.shape
    nc = l // LANE
    cap = EC_CAPACITY_FACTOR * l // N_EXPERTS
    slot, off = _topk_call(aff)
    s0 = off[:, :, 0].reshape(-1)
    ohi, olo = _expert_call(h2, slot, aff, s0, *wexp)
    return _combine_call(x1, ohi, olo, slot, s0, g2, w['ln2_g'], w['ln2_b'])


def kernel(x, c, ctx, c_ctx, w_mod, b_mod, w_in, w_gla_gate_f, b_gla_gate_f, w_gla_gate_b, b_gla_gate_b, g_gla_norm, g_mla_qa, w_mla_qb, g_mla_kva, w_mla_kvb, g_gqa_q, g_gqa_k, w_out, ln1_g, ln1_b, w_router, w_exp_gate, w_exp_up, w_exp_down, ln2_g, ln2_b):
    b, l, d = x.shape
    lc = ctx.shape[1]
    assert b < 8 and l % 256 == 0 and lc % 256 == 0
    cvec = jnp.zeros((8, d), F32).at[0:b].set(c).at[b].set(c_ctx)
    mods = _mod_call(cvec, w_mod, b_mod)
    tab_l = _tables(l, True)
    tab_c = _tables(lc, False)
    zero_state = jnp.zeros((b, GLA_WIDTH, 128), F32)
    cx = ctx
    for li in range(DEPTH):
        need_ctx = li < DEPTH - 1
        w = _prep_layer(li, w_in, w_gla_gate_f, b_gla_gate_f, w_gla_gate_b, b_gla_gate_b, g_gla_norm, g_mla_qa,
                        w_mla_qb, g_mla_kva, w_mla_kvb, g_gqa_q, g_gqa_k, w_out, ln1_g, ln1_b, w_router, ln2_g,
                        ln2_b)
        wexp = (w_exp_gate[li].astype(BF16), w_exp_up[li].astype(BF16), w_exp_down[li].astype(BF16))
        sh1, sc1, g1, sh2, sc2, g2 = [m[:, None, :] for m in jnp.split(mods[li, 0:b], 6, axis=-1)]
        csh1, csc1, cg1, csh2, csc2, cg2 = [jnp.broadcast_to(m[:, None, :], (b, 1, d))
                                            for m in jnp.split(mods[li, b:b + 1], 6, axis=-1)]
        names = ('gq', 'gk', 'la', 'gv', 'gr', 'q', 'k', 'v')
        pl_ = dict(zip(names, _proj_call(x, sc1, sh1, tab_l, w)))
        pc_ = dict(zip(names, _proj_call(cx, csc1, csh1, tab_c, w)))
        ocf, s_f = _gla_call(pc_, zero_state, 0)
        ocb, s_b = _gla_call(pc_, zero_state, 1)
        olf, _ = _gla_call(pl_, s_f, 0)
        olb, _ = _gla_call(pl_, s_b, 1)
        oatt = _attend(pl_['q'], jnp.concatenate([pc_['k'], pl_['k']], axis=1),
                       jnp.concatenate([pc_['v'], pl_['v']], axis=1))
        x1, h2, aff = _merge_call(x, olf, olb, pl_['gr'], oatt, g1, sc2, sh2, w)
        x = _moe(h2, aff, x1, g2, w, wexp)
        if need_ctx:
            oatt_c = _attend(pc_['q'], pc_['k'], pc_['v'])
            c1, ch2, caff = _merge_call(cx, ocf, ocb, pc_['gr'], oatt_c, cg1, csc2, csh2, w)
            cx = _moe(ch2, caff, c1, cg2, w, wexp)
    return x
```

```python
import functools
import math

import numpy as np
import jax
import jax.numpy as jnp
from jax import lax
from jax.experimental import pallas as pl
from jax.experimental.pallas import tpu as pltpu

F32 = jnp.float32
BF16 = jnp.bfloat16

D_MODEL = 1024
DEPTH = 4
GRID_W = 64
ROPE_THETA = 10000.0
LN_EPS = 1e-6
RMS_EPS = 1e-6
DEEPNORM_ALPHA = (2 * DEPTH) ** 0.25

GLA_HEADS, GLA_DK, GLA_DV, GLA_GATE_RANK, GLA_TAU, GLA_CHUNK = 4, 32, 64, 16, 16.0, 64
MLA_HEADS, MLA_Q_RANK, MLA_KV_RANK, MLA_NOPE, MLA_ROPE, MLA_DV = 6, 384, 256, 64, 32, 64
MLA_SCALE = (MLA_NOPE + MLA_ROPE) ** -0.5
GQA_HEADS, GQA_KV_HEADS, GQA_DH = 6, 2, 64
GQA_GROUP = GQA_HEADS // GQA_KV_HEADS
GQA_SCALE = GQA_DH ** -0.5
N_EXPERTS = 16
EC_CAPACITY_FACTOR = 2
D_EXPERT = 1024

GLA_WIDTH = GLA_HEADS * GLA_DV
IN_SPLITS = (GLA_HEADS * GLA_DK, GLA_HEADS * GLA_DK, GLA_WIDTH, GLA_WIDTH, GLA_GATE_RANK, GLA_GATE_RANK,
             MLA_Q_RANK, MLA_KV_RANK, MLA_ROPE,
             GQA_HEADS * GQA_DH, GQA_KV_HEADS * GQA_DH, GQA_KV_HEADS * GQA_DH)
IN_OFF = [0] + [int(o) for o in np.cumsum(IN_SPLITS)]

LANE = 128
HEAD_PAD = 128
N_QHEADS = MLA_HEADS + GQA_HEADS
N_KVHEADS = MLA_HEADS + GQA_KV_HEADS
ATT_DV = 64
ATT_DVP = 80
ATT_SUB = 128
ATT_AHEAD = 2
LOG2E = math.log2(math.e)
VMEM_LIMIT = 56 * 1024 * 1024

C_A, C_MQA, C_MKVA, C_AQ, C_AK, C_AV, C_G, C_AQS, C_AKS, C_GS, C_END = (
    0, 768, 1152, 1408, 2176, 2432, 2560, 2688, 3456, 3712, 3840)


def _cparams(sem):
    return pltpu.CompilerParams(dimension_semantics=sem, vmem_limit_bytes=VMEM_LIMIT)


def _dot(a, b):
    return jnp.dot(a, b, preferred_element_type=F32)


def _dot_nt(a, b):
    return lax.dot_general(a, b, (((1,), (1,)), ((), ())), preferred_element_type=F32)


def _dot_tn(a, b):
    return lax.dot_general(a, b, (((0,), (0,)), ((), ())), preferred_element_type=F32)


def _split2(x):
    hi = x.astype(BF16)
    lo = (x - hi.astype(F32)).astype(BF16)
    return hi, lo


def _split3(x):
    hi = x.astype(BF16)
    r = x - hi.astype(F32)
    mid = r.astype(BF16)
    lo = (r - mid.astype(F32)).astype(BF16)
    return hi, mid, lo


def _layer_norm(z, g, b):
    mu = jnp.mean(z, axis=-1, keepdims=True)
    zc = z - mu
    var = jnp.mean(zc * zc, axis=-1, keepdims=True)
    return zc * lax.rsqrt(var + LN_EPS) * g + b


def _mod_kernel(c_ref, w_ref, b_ref, o_ref):
    cv = c_ref[...]
    s = cv * (1.0 / (1.0 + jnp.exp(-cv)))
    o_ref[0] = _dot(s.astype(BF16), w_ref[0].astype(BF16)) + b_ref[0]


def _mod_call(cvec, w_mod, b_mod):
    depth, d, n6 = w_mod.shape
    bn = 768
    return pl.pallas_call(
        _mod_kernel,
        grid=(depth, n6 // bn),
        in_specs=[pl.BlockSpec((8, d), lambda l, j: (0, 0)),
                  pl.BlockSpec((1, d, bn), lambda l, j: (l, 0, j)),
                  pl.BlockSpec((1, 1, bn), lambda l, j: (l, 0, j))],
        out_specs=pl.BlockSpec((1, 8, bn), lambda l, j: (l, 0, j)),
        out_shape=jax.ShapeDtypeStruct((depth, 8, n6), F32),
        compiler_params=_cparams(("arbitrary", "arbitrary")),
    )(cvec, w_mod, b_mod.reshape(depth, 1, n6))


def _proj_kernel(x_ref, sc_ref, sh_ref, tab_ref, w_ref, wg_ref, bg_ref, gqa_ref, wqb_ref, gkva_ref, wkvb_ref,
                 gains_ref, bsum_ref,
                 gq_ref, gk_ref, la_ref, gv_ref, gr_ref, q_ref, k_ref, v_ref):
    h = x_ref[0] * (1.0 + sc_ref[0]) + sh_ref[0]
    hb = h.astype(BF16)

    def mm(a, b):
        return _dot(hb, w_ref[:, a:b])

    def tab(i):
        return tab_ref[:, i * LANE:(i + 1) * LANE]

    a = mm(C_A, C_MQA)
    gq_ref[0] = a[:, 0:128] * (GLA_DK ** -0.5)
    gk_ref[0] = a[:, 128:256]
    gv_ref[0] = a[:, 256:512]
    gr_ref[0] = a[:, 512:768]

    g = mm(C_G, C_AQS)
    gs = mm(C_GS, C_END)
    lg = _dot(g.astype(BF16), wg_ref[...]) + bg_ref[...]
    la_ref[0] = (jnp.minimum(lg, 0.0) - jnp.log1p(jnp.exp(-jnp.abs(lg)))) * (1.0 / GLA_TAU)

    mqa = mm(C_MQA, C_MKVA)
    nq = mqa * lax.rsqrt(jnp.mean(mqa * mqa, axis=-1, keepdims=True) + RMS_EPS) * gqa_ref[...]
    q2 = _dot(nq.astype(BF16), wqb_ref[...])
    cm, sm = tab(0), tab(1)
    for hh in range(MLA_HEADS):
        lo = hh * HEAD_PAD
        qh = q2[:, lo:lo + HEAD_PAD] * cm + q2[:, 768 + lo:768 + lo + HEAD_PAD] * sm
        q_ref[0, :, lo:lo + HEAD_PAD] = qh.astype(BF16)

    mkva = mm(C_MKVA, C_AQ)
    nkv = mkva * lax.rsqrt(jnp.mean(mkva * mkva, axis=-1, keepdims=True) + RMS_EPS) * gkva_ref[...]
    kv = _dot(nkv.astype(BF16), wkvb_ref[...])
    kr = g * tab(2) + gs * tab(3)
    for hh in range(MLA_HEADS):
        lo = hh * HEAD_PAD
        k_ref[0, :, lo:lo + HEAD_PAD] = (kv[:, lo:lo + HEAD_PAD] + kr).astype(BF16)
    v_ref[0, :, 0:384] = kv[:, 768:1152].astype(BF16)

    def head_rs(t, width):
        hi, lo = _split2(t * t)
        bs = bsum_ref[0:width, 0:width]
        ms = (_dot(hi, bs) + _dot(lo, bs)) * (1.0 / GQA_DH)
        return lax.rsqrt(ms + RMS_EPS)

    aq = mm(C_AQ, C_AK)
    aqs = mm(C_AQS, C_AKS)
    rs = head_rs(aq, 768)
    gq_g, gq_s = gains_ref[0:1, 0:768], gains_ref[1:2, 0:768]
    cg, sg = tab(4), tab(5)
    for hh in range(GQA_HEADS):
        lo = hh * HEAD_PAD
        sl = slice(lo, lo + HEAD_PAD)
        qh = aq[:, sl] * rs[:, sl] * gq_g[:, sl] * cg + aqs[:, sl] * rs[:, sl] * gq_s[:, sl] * sg
        q_ref[0, :, 768 + lo:768 + lo + HEAD_PAD] = qh.astype(BF16)

    ak = mm(C_AK, C_AV)
    aks = mm(C_AKS, C_GS)
    rsk = head_rs(ak, 256)
    gk_g, gk_s = gains_ref[2:3, 0:256], gains_ref[3:4, 0:256]
    ck, sk = tab(6), tab(7)
    for hh in range(GQA_KV_HEADS):
        lo = hh * HEAD_PAD
        sl = slice(lo, lo + HEAD_PAD)
        kh = ak[:, sl] * rsk[:, sl] * gk_g[:, sl] * ck + aks[:, sl] * rsk[:, sl] * gk_s[:, sl] * sk
        k_ref[0, :, 768 + lo:768 + lo + HEAD_PAD] = kh.astype(BF16)
    v_ref[0, :, 384:512] = mm(C_AV, C_G).astype(BF16)


def _proj_call(x, sc, sh, tab, w):
    b, l, d = x.shape
    t = min(l, 256)
    full = lambda arr: pl.BlockSpec(arr.shape, lambda bi, i: (0,) * arr.ndim)
    row = lambda n: pl.BlockSpec((1, t, n), lambda bi, i: (bi, i, 0))
    vec = pl.BlockSpec((1, 1, d), lambda bi, i: (bi, 0, 0))
    consts = [w['w_in'], w['wg'], w['bg'], w['g_qa'], w['wqb'], w['g_kva'], w['wkvb'], w['gains'], w['bsum']]
    outs = [(128, F32), (128, F32), (256, F32), (256, F32), (256, F32), (1536, BF16), (1024, BF16), (512, BF16)]
    return pl.pallas_call(
        _proj_kernel,
        grid=(b, l // t),
        in_specs=[row(d), vec, vec, pl.BlockSpec((t, 8 * LANE), lambda bi, i: (i, 0))] + [full(a) for a in consts],
        out_specs=[row(n) for n, _ in outs],
        out_shape=[jax.ShapeDtypeStruct((b, l, n), dt) for n, dt in outs],
        compiler_params=_cparams(("arbitrary", "arbitrary")),
    )(x, sc, sh, tab, *consts)


GLA_BLOCK = 256


def _gla_kernel(q_ref, k_ref, la_ref, v_ref, s0_ref, o_ref, sfin_ref, s_scr, *, reverse, nblk):
    i = pl.program_id(1)

    @pl.when(i == 0)
    def _():
        s_scr[...] = s0_ref[0]

    r = q_ref.shape[1]
    nch = r // GLA_CHUNK
    la = la_ref[0]
    q = q_ref[0]
    k = k_ref[0]
    v = v_ref[0]
    ri = lax.broadcasted_iota(jnp.int32, (r, r), 0)
    ci = lax.broadcasted_iota(jnp.int32, (r, r), 1)
    same = (ri >> 6) == (ci >> 6)
    tri = jnp.logical_and(same, (ci >= ri) if reverse else (ci <= ri))
    trib = jnp.where(tri, 1.0, 0.0).astype(BF16)
    oneb = jnp.where(same, 1.0, 0.0).astype(BF16)
    hi, mid, lo = _split3(la)
    bcum = _dot(trib, hi) + _dot(trib, mid) + _dot(trib, lo)
    tot = _dot(oneb, hi) + _dot(oneb, mid) + _dot(oneb, lo)
    qd = q * jnp.exp(bcum)
    kinv = (k * jnp.exp(-bcum)).astype(BF16)
    kend = (k * jnp.exp(tot - bcum)).astype(BF16)
    dec = jnp.exp(tot)
    vb = v.astype(BF16)
    lane_k = lax.broadcasted_iota(jnp.int32, (1, GLA_HEADS * GLA_DK), 1)
    lane_v = lax.broadcasted_iota(jnp.int32, (1, GLA_WIDTH), 1)
    o = jnp.zeros((r, GLA_WIDTH), F32)
    for hh in range(GLA_HEADS):
        qh = jnp.where((lane_k >> 5) == hh, qd, 0.0).astype(BF16)
        att = jnp.where(tri, _dot_nt(qh, kinv), 0.0)
        vh = jnp.where((lane_v >> 6) == hh, v, 0.0).astype(BF16)
        o = o + _dot(att.astype(BF16), vh)
    qdb = qd.astype(BF16)
    bd = (lax.broadcasted_iota(jnp.int32, (GLA_WIDTH, GLA_HEADS * GLA_DK), 0) >> 6) == (
        lax.broadcasted_iota(jnp.int32, (GLA_WIDTH, GLA_HEADS * GLA_DK), 1) >> 5)
    s = s_scr[...]
    order = range(nch - 1, -1, -1) if reverse else range(nch)
    for cc in order:
        rows = slice(cc * GLA_CHUNK, (cc + 1) * GLA_CHUNK)
        o_ref[0, rows, :] = o[rows] + _dot_nt(qdb[rows], s.astype(BF16))
        ut = _dot_tn(vb[rows], kend[rows])
        s = dec[cc * GLA_CHUNK:cc * GLA_CHUNK + 1, :] * s + jnp.where(bd, ut, 0.0)
    s_scr[...] = s

    @pl.when(i == nblk - 1)
    def _():
        sfin_ref[0] = s


def _gla_call(p, s0, direction):
    b, l, _ = p['gq'].shape
    r = min(l, GLA_BLOCK)
    nblk = l // r
    reverse = direction == 1
    blk = (lambda i: nblk - 1 - i) if reverse else (lambda i: i)
    row = lambda n, col=0: pl.BlockSpec((1, r, n), lambda bi, i: (bi, blk(i), col))
    st = pl.BlockSpec((1, GLA_WIDTH, 128), lambda bi, i: (bi, 0, 0))
    return pl.pallas_call(
        functools.partial(_gla_kernel, reverse=reverse, nblk=nblk),
        grid=(b, nblk),
        in_specs=[row(128), row(128), row(128, direction), row(256), st],
        out_specs=[row(256), st],
        out_shape=[jax.ShapeDtypeStruct((b, l, GLA_WIDTH), F32), jax.ShapeDtypeStruct((b, GLA_WIDTH, 128), F32)],
        scratch_shapes=[pltpu.VMEM((GLA_WIDTH, 128), F32)],
        compiler_params=_cparams(("arbitrary", "arbitrary")),
    )(p['gq'], p['gk'], p['la'], p['gv'], s0)


def _attn_kernel(qt_ref, k_ref, vt_ref, o_ref, m_scr, acc_scr, *, nk):
    ki = pl.program_id(3)

    @pl.when(ki == 0)
    def _():
        m_scr[...] = jnp.full(m_scr.shape, -jnp.inf, F32)
        acc_scr[...] = jnp.zeros(acc_scr.shape, F32)

    qt = qt_ref[0, 0]
    m, acc = m_scr[...], acc_scr[...]
    nsub = k_ref.shape[2] // ATT_SUB

    def scores(c):
        return _dot(k_ref[0, 0, c * ATT_SUB:(c + 1) * ATT_SUB, :], qt)

    def weighted_values(acc_in, alpha_c, p_c, c):
        return alpha_c * acc_in + _dot(vt_ref[0, 0, :, c * ATT_SUB:(c + 1) * ATT_SUB], p_c)

    ahead = [scores(c) for c in range(min(ATT_AHEAD, nsub))]
    pending = None
    for c in range(nsub):
        s = ahead.pop(0)
        if c + ATT_AHEAD < nsub:
            ahead.append(scores(c + ATT_AHEAD))
        m_new = jnp.maximum(m, jnp.max(s, axis=0, keepdims=True))
        alpha = jnp.exp2(m - m_new)
        p = jnp.exp2(s - m_new).astype(BF16)
        m = m_new
        if pending is not None:
            acc = weighted_values(acc, *pending)
        pending = (alpha, p, c)
    acc = weighted_values(acc, *pending)
    m_scr[...], acc_scr[...] = m, acc

    @pl.when(ki == nk - 1)
    def _():
        fin = acc_scr[...]
        o_ref[0, 0] = (fin[0:ATT_DV, :] / fin[ATT_DV:ATT_DV + 1, :]).astype(o_ref.dtype)


def _pick_tile(n, options):
    for t in options:
        if n % t == 0:
            return t
    raise ValueError(f"no tile for {n}")


def _attn_call(qt, k, vt):
    b, nh, dk, lq = qt.shape
    lk = k.shape[2]
    tq = _pick_tile(lq, (1024, 512, 256))
    tk = _pick_tile(lk, (2816, 768, 512, 256))
    nk = lk // tk

    def kvh(h):
        return jnp.where(h < MLA_HEADS, h, MLA_HEADS + (h - MLA_HEADS) // GQA_GROUP)

    return pl.pallas_call(
        functools.partial(_attn_kernel, nk=nk),
        grid=(b, nh, lq // tq, nk),
        in_specs=[pl.BlockSpec((1, 1, dk, tq), lambda bi, h, qi, ki: (bi, h, 0, qi)),
                  pl.BlockSpec((1, 1, tk, dk), lambda bi, h, qi, ki: (bi, kvh(h), ki, 0)),
                  pl.BlockSpec((1, 1, ATT_DVP, tk), lambda bi, h, qi, ki: (bi, kvh(h), 0, ki))],
        out_specs=pl.BlockSpec((1, 1, ATT_DV, tq), lambda bi, h, qi, ki: (bi, h, 0, qi)),
        out_shape=jax.ShapeDtypeStruct((b, nh, ATT_DV, lq), BF16),
        scratch_shapes=[pltpu.VMEM((1, tq), F32), pltpu.VMEM((ATT_DVP, tq), F32)],
        compiler_params=_cparams(("arbitrary", "arbitrary", "arbitrary", "arbitrary")),
    )(qt, k, vt)


def _merge_kernel(x_ref, of_ref, ob_ref, gr_ref, oa_ref, g1_ref, sc_ref, sh_ref, ggla_ref, bs_ref, wout_ref,
                  lng_ref, lnb_ref, wrh_ref, wrl_ref, x1_ref, h2_ref, aff_ref):
    o = of_ref[0] + ob_ref[0]
    hi, lo = _split2(o * o)
    ms = (_dot(hi, bs_ref[...]) + _dot(lo, bs_ref[...])) * (1.0 / GLA_DV)
    r = gr_ref[0]
    gla = o * lax.rsqrt(ms + RMS_EPS) * ggla_ref[...] * (r * (1.0 / (1.0 + jnp.exp(-r))))
    a = _dot(gla.astype(BF16), wout_ref[0:GLA_WIDTH, :]) + _dot(oa_ref[0], wout_ref[GLA_WIDTH:, :])
    x1 = _layer_norm(DEEPNORM_ALPHA * x_ref[0] + g1_ref[0] * a, lng_ref[...], lnb_ref[...])
    x1_ref[0] = x1
    h2 = x1 * (1.0 + sc_ref[0]) + sh_ref[0]
    h2h, h2l = _split2(h2)
    h2_ref[0] = h2h
    wh, wl = wrh_ref[...], wrl_ref[...]
    logits = _dot_nt(wh, h2h) + _dot_nt(wh, h2l) + _dot_nt(wl, h2h)
    e = jnp.exp(logits - jnp.max(logits, axis=0, keepdims=True))
    aff_ref[0] = e / jnp.sum(e, axis=0, keepdims=True)


def _merge_call(x, of, ob, gr, oatt, g1, sc2, sh2, w):
    b, l, d = x.shape
    t = min(l, 256)
    full = lambda arr: pl.BlockSpec(arr.shape, lambda bi, i: (0,) * arr.ndim)
    row = lambda n: pl.BlockSpec((1, t, n), lambda bi, i: (bi, i, 0))
    vec = pl.BlockSpec((1, 1, d), lambda bi, i: (bi, 0, 0))
    consts = [w['g_gla'], w['bsum64'], w['w_out'], w['ln1_g'], w['ln1_b'], w['wr_hi'], w['wr_lo']]
    return pl.pallas_call(
        _merge_kernel,
        grid=(b, l // t),
        in_specs=[row(d), row(256), row(256), row(256), row(768), vec, vec, vec] + [full(a) for a in consts],
        out_specs=[row(d), row(d), pl.BlockSpec((1, N_EXPERTS, t), lambda bi, i: (bi, 0, i))],
        out_shape=[jax.ShapeDtypeStruct((b, l, d), F32), jax.ShapeDtypeStruct((b, l, d), BF16),
                   jax.ShapeDtypeStruct((b, N_EXPERTS, l), F32)],
        compiler_params=_cparams(("arbitrary", "arbitrary")),
    )(x, of, ob, gr, oatt, g1, sc2, sh2, *consts)


def _topk_kernel(aff_ref, slot_ref, off_ref, *, nc, cap):
    shift = int(math.log2(nc))
    bits = pltpu.bitcast(aff_ref[0], jnp.int32)
    r = bits.shape[0]
    ri = lax.broadcasted_iota(jnp.int32, (r, r), 0)
    ci = lax.broadcasted_iota(jnp.int32, (r, r), 1)
    same = (ri >> shift) == (ci >> shift)
    ones_bd = jnp.where(same, 1.0, 0.0).astype(BF16)
    lstrict = jnp.where(jnp.logical_and(same, ci < ri), 1.0, 0.0).astype(BF16)
    li = lax.broadcasted_iota(jnp.int32, (LANE, LANE), 0)
    lj = lax.broadcasted_iota(jnp.int32, (LANE, LANE), 1)
    uex = jnp.where(li < lj, 1.0, 0.0).astype(BF16)
    ones = jnp.ones((LANE, LANE), BF16)

    def total(mask):
        colsum = _dot(ones_bd, jnp.where(mask, 1.0, 0.0).astype(BF16))
        return _dot(colsum.astype(BF16), ones)

    def excl_cumsum(mask):
        mb = jnp.where(mask, 1.0, 0.0).astype(BF16)
        off = _dot(lstrict, _dot(mb, ones).astype(BF16))
        return _dot(mb, uex) + off, off

    def body(it, v):
        cand = v | lax.shift_left(jnp.int32(1), jnp.int32(30) - it)
        return jnp.where(total(bits >= cand) >= cap, cand, v)

    v = lax.fori_loop(0, 31, body, jnp.zeros(bits.shape, jnp.int32))
    gt = bits > v
    eq = bits == v
    need = cap - total(gt)
    rank_eq, _ = excl_cumsum(eq)
    sel = jnp.logical_or(gt, jnp.logical_and(eq, rank_eq < need))
    pos, off = excl_cumsum(sel)
    slot_ref[0] = jnp.where(sel, pos.astype(jnp.int32), -1)
    off_ref[0] = off.astype(jnp.int32)


def _topk_call(aff):
    b, e, l = aff.shape
    nc = l // LANE
    cap = EC_CAPACITY_FACTOR * l // N_EXPERTS
    r = e * nc
    spec = pl.BlockSpec((1, r, LANE), lambda bi: (bi, 0, 0))
    return pl.pallas_call(
        functools.partial(_topk_kernel, nc=nc, cap=cap),
        grid=(b,),
        in_specs=[spec],
        out_specs=[spec, spec],
        out_shape=[jax.ShapeDtypeStruct((b, r, LANE), jnp.int32)] * 2,
        compiler_params=_cparams(("arbitrary",)),
    )(aff.reshape(b, r, LANE))


def _expert_kernel(s0_ref, h_ref, slot_ref, aff_ref, wg_ref, wu_ref, wd_ref, ohi_ref, olo_ref, xs_scr, gate_scr,
                   *, nb, nsub, nc, cap, win):
    e, b, j = pl.program_id(0), pl.program_id(1), pl.program_id(2)

    @pl.when(j == 0)
    def _():
        xs_scr[...] = jnp.zeros(xs_scr.shape, F32)
        gate_scr[...] = jnp.zeros(gate_scr.shape, F32)

    base = (b * N_EXPERTS + e) * nc + j * nsub
    rows = lax.broadcasted_iota(jnp.int32, (win, LANE), 0)
    for u in range(nsub):
        s0a = pl.multiple_of((s0_ref[base + u] >> 3) << 3, 8)
        lanes = slice(u * LANE, (u + 1) * LANE)
        hit = (rows + s0a) == slot_ref[0, :, lanes]
        comp = _dot(jnp.where(hit, 1.0, 0.0).astype(BF16), h_ref[0, lanes, :])
        xs_scr[pl.ds(s0a, win), :] += comp
        gate_scr[pl.ds(s0a, win), :] += jnp.sum(jnp.where(hit, aff_ref[0, :, lanes], 0.0), axis=1, keepdims=True)

    @pl.when(j == nb - 1)
    def _():
        rc = min(cap, 256)
        for c0 in range(0, cap, rc):
            xs = xs_scr[c0:c0 + rc, :].astype(BF16)
            gg = _dot(xs, wg_ref[0])
            hid = (gg * (1.0 / (1.0 + jnp.exp(-gg)))) * _dot(xs, wu_ref[0])
            out = _dot(hid.astype(BF16), wd_ref[0]) * gate_scr[c0:c0 + rc, :]
            hi, lo = _split2(out)
            ohi_ref[0, 0, c0:c0 + rc, :] = hi
            olo_ref[0, 0, c0:c0 + rc, :] = lo
        pad = jnp.zeros((ohi_ref.shape[2] - cap, ohi_ref.shape[3]), BF16)
        ohi_ref[0, 0, cap:, :] = pad
        olo_ref[0, 0, cap:, :] = pad


def _expert_call(h2, slot, aff, s0, wg, wu, wd):
    b, l, d = h2.shape
    nc = l // LANE
    cap = EC_CAPACITY_FACTOR * l // N_EXPERTS
    tb = min(l, 2048)
    nsub = tb // LANE
    nb = l // tb
    win = LANE + 8
    f = wg.shape[2]
    row3 = lambda e, bi, j, s: ((bi * N_EXPERTS + e) * nb + j, 0, 0)
    grid_spec = pltpu.PrefetchScalarGridSpec(
        num_scalar_prefetch=1,
        grid=(N_EXPERTS, b, nb),
        in_specs=[pl.BlockSpec((1, tb, d), lambda e, bi, j, s: (bi, j, 0)),
                  pl.BlockSpec((1, 1, tb), row3),
                  pl.BlockSpec((1, 1, tb), row3),
                  pl.BlockSpec((1, d, f), lambda e, bi, j, s: (e, 0, 0)),
                  pl.BlockSpec((1, d, f), lambda e, bi, j, s: (e, 0, 0)),
                  pl.BlockSpec((1, f, d), lambda e, bi, j, s: (e, 0, 0))],
        out_specs=[pl.BlockSpec((1, 1, cap + COMBINE_WIN, d), lambda e, bi, j, s: (bi, e, 0, 0))] * 2,
        scratch_shapes=[pltpu.VMEM((cap + win, d), F32), pltpu.VMEM((cap + win, 1), F32)],
    )
    return pl.pallas_call(
        functools.partial(_expert_kernel, nb=nb, nsub=nsub, nc=nc, cap=cap, win=win),
        grid_spec=grid_spec,
        out_shape=[jax.ShapeDtypeStruct((b, N_EXPERTS, cap + COMBINE_WIN, d), BF16)] * 2,
        compiler_params=_cparams(("arbitrary", "arbitrary", "arbitrary")),
    )(s0, h2, slot.reshape(b * N_EXPERTS * nb, 1, tb), aff.reshape(b * N_EXPERTS * nb, 1, tb), wg, wu, wd)


COMBINE_WIN = LANE + 16


COMBINE_BASE = 48
COMBINE_EXT = COMBINE_WIN - COMBINE_BASE


def _combine_kernel(s0_ref, end_ref, ohi_hbm, olo_hbm, slot_ref, x1_ref, g2_ref, lng_ref, lnb_ref, o_ref,
                    hbase, lbase, hext, lext, sem, *, nc, nsteps):
    b, j = pl.program_id(0), pl.program_id(1)
    t = b * nc + j
    cur = t % 2

    def win_start(bb, jj, e):
        return pl.multiple_of((s0_ref[(bb * N_EXPERTS + e) * nc + jj] >> 4) << 4, 16)

    def needs_ext(bb, jj, e):
        return end_ref[(bb * N_EXPERTS + e) * nc + jj] > win_start(bb, jj, e) + COMBINE_BASE

    def base_copies(bb, jj, sl):
        out = []
        for e in range(N_EXPERTS):
            rows = pl.ds(win_start(bb, jj, e), COMBINE_BASE)
            dst = pl.ds(e * COMBINE_BASE, COMBINE_BASE)
            out.append(pltpu.make_async_copy(ohi_hbm.at[bb, e, rows, :], hbase.at[sl, dst, :], sem.at[sl, 0]))
            out.append(pltpu.make_async_copy(olo_hbm.at[bb, e, rows, :], lbase.at[sl, dst, :], sem.at[sl, 1]))
        return out

    def ext_copies(bb, jj, sl, e):
        rows = pl.ds(win_start(bb, jj, e) + COMBINE_BASE, COMBINE_EXT)
        dst = pl.ds(e * COMBINE_EXT, COMBINE_EXT)
        return [pltpu.make_async_copy(ohi_hbm.at[bb, e, rows, :], hext.at[sl, dst, :], sem.at[sl, 2]),
                pltpu.make_async_copy(olo_hbm.at[bb, e, rows, :], lext.at[sl, dst, :], sem.at[sl, 3])]

    def start_all(bb, jj, sl):
        for cp in base_copies(bb, jj, sl):
            cp.start()
        for e in range(N_EXPERTS):
            @pl.when(needs_ext(bb, jj, e))
            def _():
                for cp in ext_copies(bb, jj, sl, e):
                    cp.start()

    @pl.when(t == 0)
    def _():
        hext[...] = jnp.zeros(hext.shape, BF16)
        lext[...] = jnp.zeros(lext.shape, BF16)
        start_all(b, j, cur)

    @pl.when(t + 1 < nsteps)
    def _():
        wrap = j + 1 == nc
        start_all(jnp.where(wrap, b + 1, b), jnp.where(wrap, 0, j + 1), 1 - cur)

    for cp in base_copies(b, j, cur):
        cp.wait()
    any_ext = needs_ext(b, j, 0)
    for e in range(N_EXPERTS):
        any_ext = jnp.logical_or(any_ext, needs_ext(b, j, e))

        @pl.when(needs_ext(b, j, e))
        def _():
            for cp in ext_copies(b, j, cur, e):
                cp.wait()

    def onehot(nrows, offset):
        rows = lax.broadcasted_iota(jnp.int32, (nrows, LANE), 0) + offset
        return jnp.concatenate(
            [jnp.where((rows + win_start(b, j, e)) == slot_ref[0, e:e + 1, :], 1.0, 0.0).astype(BF16)
             for e in range(N_EXPERTS)], axis=0)

    p = onehot(COMBINE_BASE, 0)
    z = DEEPNORM_ALPHA * x1_ref[0] + g2_ref[0] * (_dot_tn(p, hbase[cur]) + _dot_tn(p, lbase[cur]))
    o_ref[0] = _layer_norm(z, lng_ref[...], lnb_ref[...])

    @pl.when(any_ext)
    def _():
        pe = onehot(COMBINE_EXT, COMBINE_BASE)
        z2 = z + g2_ref[0] * (_dot_tn(pe, hext[cur]) + _dot_tn(pe, lext[cur]))
        o_ref[0] = _layer_norm(z2, lng_ref[...], lnb_ref[...])


def _combine_call(x1, ohi, olo, slot, s0, end, g2, lng, lnb):
    b, l, d = x1.shape
    nc = l // LANE
    full = lambda arr: pl.BlockSpec(arr.shape, lambda bi, j, s, en: (0,) * arr.ndim)
    grid_spec = pltpu.PrefetchScalarGridSpec(
        num_scalar_prefetch=2,
        grid=(b, nc),
        in_specs=[pl.BlockSpec(memory_space=pl.ANY), pl.BlockSpec(memory_space=pl.ANY),
                  pl.BlockSpec((1, N_EXPERTS, LANE), lambda bi, j, s, en: (bi, 0, j)),
                  pl.BlockSpec((1, LANE, d), lambda bi, j, s, en: (bi, j, 0)),
                  pl.BlockSpec((1, 1, d), lambda bi, j, s, en: (bi, 0, 0)),
                  full(lng), full(lnb)],
        out_specs=pl.BlockSpec((1, LANE, d), lambda bi, j, s, en: (bi, j, 0)),
        scratch_shapes=[pltpu.VMEM((2, N_EXPERTS * COMBINE_BASE, d), BF16),
                        pltpu.VMEM((2, N_EXPERTS * COMBINE_BASE, d), BF16),
                        pltpu.VMEM((2, N_EXPERTS * COMBINE_EXT, d), BF16),
                        pltpu.VMEM((2, N_EXPERTS * COMBINE_EXT, d), BF16),
                        pltpu.SemaphoreType.DMA((2, 4))],
    )
    return pl.pallas_call(
        functools.partial(_combine_kernel, nc=nc, nsteps=b * nc),
        grid_spec=grid_spec,
        out_shape=jax.ShapeDtypeStruct((b, l, d), F32),
        compiler_params=_cparams(("arbitrary", "arbitrary")),
    )(s0, end, ohi, olo, slot.reshape(b, N_EXPERTS, l), x1, g2, lng, lnb)


def _pad_heads(w, nheads, dh):
    return jnp.pad(w.reshape(w.shape[0], nheads, dh), ((0, 0), (0, 0), (0, HEAD_PAD - dh))).reshape(w.shape[0], -1)


def _swap_halves(w, nheads, dh):
    q = dh // 4
    return jnp.flip(w.reshape(w.shape[0], nheads, 2, 2, q), axis=3).reshape(w.shape[0], nheads * dh)


def _prep_layer(l, w_in, w_gla_gate_f, b_gla_gate_f, w_gla_gate_b, b_gla_gate_b, g_gla_norm, g_mla_qa, w_mla_qb,
                g_mla_kva, w_mla_kvb, g_gqa_q, g_gqa_k, w_out, ln1_g, ln1_b, w_router, ln2_g, ln2_b):
    wi = w_in[l]
    d = wi.shape[0]
    o = IN_OFF
    aq, ak, av = wi[:, o[9]:o[10]], wi[:, o[10]:o[11]], wi[:, o[11]:o[12]]
    mkr = wi[:, o[8]:o[9]]
    zeros = lambda n: jnp.zeros((d, n), F32)
    g_blk = jnp.concatenate([wi[:, o[4]:o[6]], zeros(32), mkr, zeros(32)], axis=1)
    gs_blk = jnp.concatenate([zeros(64), _swap_halves(mkr, 1, MLA_ROPE), zeros(32)], axis=1)
    w_in_r = jnp.concatenate([
        wi[:, 0:o[4]], wi[:, o[6]:o[7]], wi[:, o[7]:o[8]],
        _pad_heads(aq, GQA_HEADS, GQA_DH), _pad_heads(ak, GQA_KV_HEADS, GQA_DH), av, g_blk,
        _pad_heads(_swap_halves(aq, GQA_HEADS, GQA_DH), GQA_HEADS, GQA_DH),
        _pad_heads(_swap_halves(ak, GQA_KV_HEADS, GQA_DH), GQA_KV_HEADS, GQA_DH), gs_blk], axis=1).astype(BF16)
    assert w_in_r.shape[1] == C_END

    r = GLA_GATE_RANK
    wg = jnp.zeros((LANE, 256), F32)
    wg = wg.at[0:r, 0:128].set(w_gla_gate_f[l]).at[r:2 * r, 128:256].set(w_gla_gate_b[l]).astype(BF16)
    bg = jnp.concatenate([b_gla_gate_f[l], b_gla_gate_b[l]])[None, :]

    qb = w_mla_qb[l].reshape(MLA_Q_RANK, MLA_HEADS, MLA_NOPE + MLA_ROPE)
    qb_rope_sw = _swap_halves(qb[:, :, MLA_NOPE:].reshape(MLA_Q_RANK, -1), MLA_HEADS, MLA_ROPE).reshape(
        MLA_Q_RANK, MLA_HEADS, MLA_ROPE)
    qb_sw = jnp.concatenate([jnp.zeros_like(qb[:, :, :MLA_NOPE]), qb_rope_sw], axis=2)
    padq = lambda t: jnp.pad(t, ((0, 0), (0, 0), (0, HEAD_PAD - t.shape[2]))).reshape(MLA_Q_RANK, -1)
    wqb = jnp.concatenate([padq(qb), padq(qb_sw)], axis=1).astype(BF16)

    kvb = w_mla_kvb[l].reshape(MLA_KV_RANK, MLA_HEADS, MLA_NOPE + MLA_DV)
    kpart = jnp.pad(kvb[:, :, :MLA_NOPE], ((0, 0), (0, 0), (0, HEAD_PAD - MLA_NOPE))).reshape(MLA_KV_RANK, -1)
    wkvb = jnp.concatenate([kpart, kvb[:, :, MLA_NOPE:].reshape(MLA_KV_RANK, -1)], axis=1).astype(BF16)

    def gain_rows(gv, nheads):
        g2 = jnp.tile(gv[None, :], (1, nheads))
        plain = _pad_heads(g2, nheads, GQA_DH)
        swapped = _pad_heads(_swap_halves(g2, nheads, GQA_DH), nheads, GQA_DH)
        return [jnp.pad(t, ((0, 0), (0, 768 - t.shape[1]))) for t in (plain, swapped)]

    gains = jnp.concatenate(gain_rows(g_gqa_q[l], GQA_HEADS) + gain_rows(g_gqa_k[l], GQA_KV_HEADS)
                            + [jnp.zeros((4, 768), F32)], axis=0)
    ii = np.arange(768)
    bsum = jnp.asarray((ii[:, None] // HEAD_PAD) == (ii[None, :] // HEAD_PAD), BF16)
    jj = np.arange(GLA_WIDTH)
    bsum64 = jnp.asarray((jj[:, None] // GLA_DV) == (jj[None, :] // GLA_DV), BF16)
    wr = w_router[l].T
    wr_hi = wr.astype(BF16)
    wr_lo = (wr - wr_hi.astype(F32)).astype(BF16)
    return dict(w_in=w_in_r, wg=wg, bg=bg, g_qa=g_mla_qa[l][None, :], wqb=wqb, g_kva=g_mla_kva[l][None, :],
                wkvb=wkvb, gains=gains, bsum=bsum, g_gla=jnp.tile(g_gla_norm[l][None, :], (1, GLA_HEADS)),
                bsum64=bsum64, w_out=w_out[l].astype(BF16), ln1_g=ln1_g[l][None, :], ln1_b=ln1_b[l][None, :],
                wr_hi=wr_hi, wr_lo=wr_lo, ln2_g=ln2_g[l][None, :], ln2_b=ln2_b[l][None, :])


def _tables(l, with_pos):
    sm = MLA_SCALE * LOG2E
    sg = GQA_SCALE * LOG2E
    z = lambda n: jnp.zeros((l, n), F32)
    one = lambda n: jnp.ones((l, n), F32)
    if with_pos:
        t = jnp.arange(l, dtype=jnp.int32)
        row = (t // GRID_W).astype(F32)[:, None]
        col = (t % GRID_W).astype(F32)[:, None]

        def cs(nfreq):
            inv = ROPE_THETA ** (-jnp.arange(nfreq, dtype=F32) / nfreq)
            ar, ac = row * inv, col * inv
            c = jnp.concatenate([jnp.cos(ar), jnp.cos(ar), jnp.cos(ac), jnp.cos(ac)], axis=1)
            s = jnp.concatenate([-jnp.sin(ar), jnp.sin(ar), -jnp.sin(ac), jnp.sin(ac)], axis=1)
            return c, s

        c32, s32 = cs(MLA_ROPE // 4)
        c64, s64 = cs(GQA_DH // 4)
    else:
        c32, s32, c64, s64 = one(32), z(32), one(64), z(64)
    cm = jnp.concatenate([one(64), c32, z(32)], axis=1) * sm
    smm = jnp.concatenate([z(64), s32, z(32)], axis=1) * sm
    ck = jnp.concatenate([z(64), c32, z(32)], axis=1)
    sk = jnp.concatenate([z(64), s32, z(32)], axis=1)
    cg = jnp.concatenate([c64, z(64)], axis=1)
    sgg = jnp.concatenate([s64, z(64)], axis=1)
    return jnp.concatenate([cm, smm, ck, sk, cg * sg, sgg * sg, cg, sgg], axis=1)


def _heads_t(a, nheads, dh):
    b, l, _ = a.shape
    return a.reshape(b, l, nheads, dh).transpose(0, 2, 3, 1)


def _heads(a, nheads, dh):
    b, l, _ = a.shape
    return a.reshape(b, l, nheads, dh).transpose(0, 2, 1, 3)


def _attend(q, k, v):
    b, lq, _ = q.shape
    vt = _heads_t(v, N_KVHEADS, ATT_DV)
    extra = jnp.zeros(vt.shape[:2] + (ATT_DVP - ATT_DV, vt.shape[3]), BF16).at[:, :, 0, :].set(1.0)
    ot = _attn_call(_heads_t(q, N_QHEADS, HEAD_PAD), _heads(k, N_KVHEADS, HEAD_PAD),
                    jnp.concatenate([vt, extra], axis=2))
    return ot.transpose(0, 3, 1, 2).reshape(b, lq, N_QHEADS * ATT_DV)


def _moe(h2, aff, x1, g2, w, wexp):
    b, l, _ = h2.shape
    nc = l // LANE
    cap = EC_CAPACITY_FACTOR * l // N_EXPERTS
    slot, off = _topk_call(aff)
    s0 = off[:, :, 0].reshape(-1)
    ohi, olo = _expert_call(h2, slot, aff, s0, *wexp)
    starts = s0.reshape(b, N_EXPERTS, nc)
    end = jnp.concatenate([starts[:, :, 1:], jnp.full((b, N_EXPERTS, 1), cap, jnp.int32)], axis=2).reshape(-1)
    return _combine_call(x1, ohi, olo, slot, s0, end, g2, w['ln2_g'], w['ln2_b'])


def kernel(x, c, ctx, c_ctx, w_mod, b_mod, w_in, w_gla_gate_f, b_gla_gate_f, w_gla_gate_b, b_gla_gate_b, g_gla_norm, g_mla_qa, w_mla_qb, g_mla_kva, w_mla_kvb, g_gqa_q, g_gqa_k, w_out, ln1_g, ln1_b, w_router, w_exp_gate, w_exp_up, w_exp_down, ln2_g, ln2_b):
    b, l, d = x.shape
    lc = ctx.shape[1]
    assert b < 8 and l % 256 == 0 and lc % 256 == 0
    cvec = jnp.zeros((8, d), F32).at[0:b].set(c).at[b].set(c_ctx)
    mods = _mod_call(cvec, w_mod, b_mod)
    tab_l = _tables(l, True)
    tab_c = _tables(lc, False)
    zero_state = jnp.zeros((b, GLA_WIDTH, 128), F32)
    cx = ctx
    for li in range(DEPTH):
        need_ctx = li < DEPTH - 1
        w = _prep_layer(li, w_in, w_gla_gate_f, b_gla_gate_f, w_gla_gate_b, b_gla_gate_b, g_gla_norm, g_mla_qa,
                        w_mla_qb, g_mla_kva, w_mla_kvb, g_gqa_q, g_gqa_k, w_out, ln1_g, ln1_b, w_router, ln2_g,
                        ln2_b)
        wexp = (w_exp_gate[li].astype(BF16), w_exp_up[li].astype(BF16), w_exp_down[li].astype(BF16))
        sh1, sc1, g1, sh2, sc2, g2 = [m[:, None, :] for m in jnp.split(mods[li, 0:b], 6, axis=-1)]
        csh1, csc1, cg1, csh2, csc2, cg2 = [jnp.broadcast_to(m[:, None, :], (b, 1, d))
                                            for m in jnp.split(mods[li, b:b + 1], 6, axis=-1)]
        names = ('gq', 'gk', 'la', 'gv', 'gr', 'q', 'k', 'v')
        pl_ = dict(zip(names, _proj_call(x, sc1, sh1, tab_l, w)))
        pc_ = dict(zip(names, _proj_call(cx, csc1, csh1, tab_c, w)))
        ocf, s_f = _gla_call(pc_, zero_state, 0)
        ocb, s_b = _gla_call(pc_, zero_state, 1)
        olf, _ = _gla_call(pl_, s_f, 0)
        olb, _ = _gla_call(pl_, s_b, 1)
        oatt = _attend(pl_['q'], jnp.concatenate([pc_['k'], pl_['k']], axis=1),
                       jnp.concatenate([pc_['v'], pl_['v']], axis=1))
        x1, h2, aff = _merge_call(x, olf, olb, pl_['gr'], oatt, g1, sc2, sh2, w)
        x = _moe(h2, aff, x1, g2, w, wexp)
        if need_ctx:
            oatt_c = _attend(pc_['q'], pc_['k'], pc_['v'])
            c1, ch2, caff = _merge_call(cx, ocf, ocb, pc_['gr'], oatt_c, cg1, csc2, csh2, w)
            cx = _moe(ch2, caff, c1, cg2, w, wexp)
    return x
```

```python
import functools
import math

import numpy as np
import jax
import jax.numpy as jnp
from jax import lax
from jax.experimental import pallas as pl
from jax.experimental.pallas import tpu as pltpu

F32 = jnp.float32
BF16 = jnp.bfloat16

D_MODEL = 1024
DEPTH = 4
GRID_W = 64
ROPE_THETA = 10000.0
LN_EPS = 1e-6
RMS_EPS = 1e-6
DEEPNORM_ALPHA = (2 * DEPTH) ** 0.25

GLA_HEADS, GLA_DK, GLA_DV, GLA_GATE_RANK, GLA_TAU, GLA_CHUNK = 4, 32, 64, 16, 16.0, 64
MLA_HEADS, MLA_Q_RANK, MLA_KV_RANK, MLA_NOPE, MLA_ROPE, MLA_DV = 6, 384, 256, 64, 32, 64
MLA_SCALE = (MLA_NOPE + MLA_ROPE) ** -0.5
GQA_HEADS, GQA_KV_HEADS, GQA_DH = 6, 2, 64
GQA_GROUP = GQA_HEADS // GQA_KV_HEADS
GQA_SCALE = GQA_DH ** -0.5
N_EXPERTS = 16
EC_CAPACITY_FACTOR = 2
D_EXPERT = 1024

GLA_WIDTH = GLA_HEADS * GLA_DV
IN_SPLITS = (GLA_HEADS * GLA_DK, GLA_HEADS * GLA_DK, GLA_WIDTH, GLA_WIDTH, GLA_GATE_RANK, GLA_GATE_RANK,
             MLA_Q_RANK, MLA_KV_RANK, MLA_ROPE,
             GQA_HEADS * GQA_DH, GQA_KV_HEADS * GQA_DH, GQA_KV_HEADS * GQA_DH)
IN_OFF = [0] + [int(o) for o in np.cumsum(IN_SPLITS)]

LANE = 128
HEAD_PAD = 128
N_QHEADS = MLA_HEADS + GQA_HEADS
N_KVHEADS = MLA_HEADS + GQA_KV_HEADS
ATT_DV = 64
ATT_DVP = 80
ATT_SUB = 128
ATT_AHEAD = 2
LOG2E = math.log2(math.e)
VMEM_LIMIT = 56 * 1024 * 1024

C_A, C_MQA, C_MKVA, C_AQ, C_AK, C_AV, C_G, C_AQS, C_AKS, C_GS, C_END = (
    0, 768, 1152, 1408, 1792, 1920, 2048, 2176, 2560, 2688, 2816)


def _cparams(sem):
    return pltpu.CompilerParams(dimension_semantics=sem, vmem_limit_bytes=VMEM_LIMIT)


def _dot(a, b):
    return jnp.dot(a, b, preferred_element_type=F32)


def _dot_nt(a, b):
    return lax.dot_general(a, b, (((1,), (1,)), ((), ())), preferred_element_type=F32)


def _dot_tn(a, b):
    return lax.dot_general(a, b, (((0,), (0,)), ((), ())), preferred_element_type=F32)


def _split2(x):
    hi = x.astype(BF16)
    lo = (x - hi.astype(F32)).astype(BF16)
    return hi, lo


def _split3(x):
    hi = x.astype(BF16)
    r = x - hi.astype(F32)
    mid = r.astype(BF16)
    lo = (r - mid.astype(F32)).astype(BF16)
    return hi, mid, lo


def _layer_norm(z, g, b):
    mu = jnp.mean(z, axis=-1, keepdims=True)
    zc = z - mu
    var = jnp.mean(zc * zc, axis=-1, keepdims=True)
    return zc * lax.rsqrt(var + LN_EPS) * g + b


def _mod_kernel(c_ref, w_ref, b_ref, o_ref):
    cv = c_ref[...]
    s = cv * (1.0 / (1.0 + jnp.exp(-cv)))
    o_ref[0] = _dot(s.astype(BF16), w_ref[0].astype(BF16)) + b_ref[0]


def _mod_call(cvec, w_mod, b_mod):
    depth, d, n6 = w_mod.shape
    bn = 768
    return pl.pallas_call(
        _mod_kernel,
        grid=(depth, n6 // bn),
        in_specs=[pl.BlockSpec((8, d), lambda l, j: (0, 0)),
                  pl.BlockSpec((1, d, bn), lambda l, j: (l, 0, j)),
                  pl.BlockSpec((1, 1, bn), lambda l, j: (l, 0, j))],
        out_specs=pl.BlockSpec((1, 8, bn), lambda l, j: (l, 0, j)),
        out_shape=jax.ShapeDtypeStruct((depth, 8, n6), F32),
        compiler_params=_cparams(("arbitrary", "arbitrary")),
    )(cvec, w_mod, b_mod.reshape(depth, 1, n6))


def _proj_kernel(x_ref, sc_ref, sh_ref, tab_ref, w_ref, wg_ref, bg_ref, gqa_ref, wqb_ref, gkva_ref, wkvb_ref,
                 gains_ref,
                 gq_ref, gk_ref, la_ref, gv_ref, gr_ref, q_ref, qg_ref, k_ref, kg_ref, v_ref):
    h = x_ref[0] * (1.0 + sc_ref[0]) + sh_ref[0]
    hb = h.astype(BF16)

    def mm(a, b):
        return _dot(hb, w_ref[:, a:b])

    def tab(i):
        return tab_ref[:, i * LANE:(i + 1) * LANE]

    a = mm(C_A, C_MQA)
    gq_ref[0] = a[:, 0:128] * (GLA_DK ** -0.5)
    gk_ref[0] = a[:, 128:256]
    gv_ref[0] = a[:, 256:512]
    gr_ref[0] = a[:, 512:768]

    g = mm(C_G, C_AQS)
    gs = mm(C_GS, C_END)
    lg = _dot(g.astype(BF16), wg_ref[...]) + bg_ref[...]
    la_ref[0] = (jnp.minimum(lg, 0.0) - jnp.log1p(jnp.exp(-jnp.abs(lg)))) * (1.0 / GLA_TAU)

    mqa = mm(C_MQA, C_MKVA)
    nq = mqa * lax.rsqrt(jnp.mean(mqa * mqa, axis=-1, keepdims=True) + RMS_EPS) * gqa_ref[...]
    q2 = _dot(nq.astype(BF16), wqb_ref[...])
    cm, sm = tab(0), tab(1)
    for hh in range(MLA_HEADS):
        lo = hh * HEAD_PAD
        qh = q2[:, lo:lo + HEAD_PAD] * cm + q2[:, 768 + lo:768 + lo + HEAD_PAD] * sm
        q_ref[0, :, lo:lo + HEAD_PAD] = qh.astype(BF16)

    mkva = mm(C_MKVA, C_AQ)
    nkv = mkva * lax.rsqrt(jnp.mean(mkva * mkva, axis=-1, keepdims=True) + RMS_EPS) * gkva_ref[...]
    kv = _dot(nkv.astype(BF16), wkvb_ref[...])
    kr = g * tab(2) + gs * tab(3)
    for hh in range(MLA_HEADS):
        lo = hh * HEAD_PAD
        k_ref[0, :, lo:lo + HEAD_PAD] = (kv[:, lo:lo + HEAD_PAD] + kr).astype(BF16)
    v_ref[0, :, 0:384] = kv[:, 768:1152].astype(BF16)

    first = lax.broadcasted_iota(jnp.int32, (1, LANE), 1) < GQA_DH

    def pair_rs(t):
        sq = t * t
        s_lo = jnp.sum(jnp.where(first, sq, 0.0), axis=-1, keepdims=True)
        s_hi = jnp.sum(jnp.where(first, 0.0, sq), axis=-1, keepdims=True)
        return lax.rsqrt(jnp.where(first, s_lo, s_hi) * (1.0 / GQA_DH) + RMS_EPS)

    def normed_rotary(x, xs, g_row, gs_row, cos, sin):
        rs = pair_rs(x)
        return x * rs * g_row * cos + xs * rs * gs_row * sin

    aq = mm(C_AQ, C_AK)
    aqs = mm(C_AQS, C_AKS)
    for pp in range(GQA_HEADS // 2):
        sl = slice(pp * LANE, (pp + 1) * LANE)
        qh = normed_rotary(aq[:, sl], aqs[:, sl], gains_ref[0:1, sl], gains_ref[1:2, sl], tab(4), tab(5))
        qg_ref[0, :, sl] = qh.astype(BF16)
    kh = normed_rotary(mm(C_AK, C_AV), mm(C_AKS, C_GS), gains_ref[2:3, 0:LANE], gains_ref[3:4, 0:LANE],
                       tab(6), tab(7))
    kg_ref[0] = kh.astype(BF16)
    v_ref[0, :, 384:512] = mm(C_AV, C_G).astype(BF16)


def _proj_call(x, sc, sh, tab, w):
    b, l, d = x.shape
    t = min(l, 256)
    full = lambda arr: pl.BlockSpec(arr.shape, lambda bi, i: (0,) * arr.ndim)
    row = lambda n: pl.BlockSpec((1, t, n), lambda bi, i: (bi, i, 0))
    vec = pl.BlockSpec((1, 1, d), lambda bi, i: (bi, 0, 0))
    consts = [w['w_in'], w['wg'], w['bg'], w['g_qa'], w['wqb'], w['g_kva'], w['wkvb'], w['gains']]
    outs = [(128, F32), (128, F32), (256, F32), (256, F32), (256, F32),
            (768, BF16), (384, BF16), (768, BF16), (128, BF16), (512, BF16)]
    return pl.pallas_call(
        _proj_kernel,
        grid=(b, l // t),
        in_specs=[row(d), vec, vec, pl.BlockSpec((t, 8 * LANE), lambda bi, i: (i, 0))] + [full(a) for a in consts],
        out_specs=[row(n) for n, _ in outs],
        out_shape=[jax.ShapeDtypeStruct((b, l, n), dt) for n, dt in outs],
        compiler_params=_cparams(("arbitrary", "arbitrary")),
    )(x, sc, sh, tab, *consts)


GLA_BLOCK = 256


def _gla_kernel(q_ref, k_ref, la_ref, v_ref, s0_ref, o_ref, sfin_ref, s_scr, *, reverse, nblk):
    i = pl.program_id(1)

    @pl.when(i == 0)
    def _():
        s_scr[...] = s0_ref[0]

    r = q_ref.shape[1]
    nch = r // GLA_CHUNK
    la = la_ref[0]
    q = q_ref[0]
    k = k_ref[0]
    v = v_ref[0]
    ri = lax.broadcasted_iota(jnp.int32, (r, r), 0)
    ci = lax.broadcasted_iota(jnp.int32, (r, r), 1)
    same = (ri >> 6) == (ci >> 6)
    tri = jnp.logical_and(same, (ci >= ri) if reverse else (ci <= ri))
    trib = jnp.where(tri, 1.0, 0.0).astype(BF16)
    oneb = jnp.where(same, 1.0, 0.0).astype(BF16)
    hi, mid, lo = _split3(la)
    bcum = _dot(trib, hi) + _dot(trib, mid) + _dot(trib, lo)
    tot = _dot(oneb, hi) + _dot(oneb, mid) + _dot(oneb, lo)
    qd = q * jnp.exp(bcum)
    kinv = (k * jnp.exp(-bcum)).astype(BF16)
    kend = (k * jnp.exp(tot - bcum)).astype(BF16)
    dec = jnp.exp(tot)
    vb = v.astype(BF16)
    lane_k = lax.broadcasted_iota(jnp.int32, (1, GLA_HEADS * GLA_DK), 1)
    lane_v = lax.broadcasted_iota(jnp.int32, (1, GLA_WIDTH), 1)
    o = jnp.zeros((r, GLA_WIDTH), F32)
    for hh in range(GLA_HEADS):
        qh = jnp.where((lane_k >> 5) == hh, qd, 0.0).astype(BF16)
        att = jnp.where(tri, _dot_nt(qh, kinv), 0.0)
        vh = jnp.where((lane_v >> 6) == hh, v, 0.0).astype(BF16)
        o = o + _dot(att.astype(BF16), vh)
    qdb = qd.astype(BF16)
    bd = (lax.broadcasted_iota(jnp.int32, (GLA_WIDTH, GLA_HEADS * GLA_DK), 0) >> 6) == (
        lax.broadcasted_iota(jnp.int32, (GLA_WIDTH, GLA_HEADS * GLA_DK), 1) >> 5)
    s = s_scr[...]
    order = range(nch - 1, -1, -1) if reverse else range(nch)
    for cc in order:
        rows = slice(cc * GLA_CHUNK, (cc + 1) * GLA_CHUNK)
        o_ref[0, rows, :] = o[rows] + _dot_nt(qdb[rows], s.astype(BF16))
        ut = _dot_tn(vb[rows], kend[rows])
        s = dec[cc * GLA_CHUNK:cc * GLA_CHUNK + 1, :] * s + jnp.where(bd, ut, 0.0)
    s_scr[...] = s

    @pl.when(i == nblk - 1)
    def _():
        sfin_ref[0] = s


def _gla_call(p, s0, direction):
    b, l, _ = p['gq'].shape
    r = min(l, GLA_BLOCK)
    nblk = l // r
    reverse = direction == 1
    blk = (lambda i: nblk - 1 - i) if reverse else (lambda i: i)
    row = lambda n, col=0: pl.BlockSpec((1, r, n), lambda bi, i: (bi, blk(i), col))
    st = pl.BlockSpec((1, GLA_WIDTH, 128), lambda bi, i: (bi, 0, 0))
    return pl.pallas_call(
        functools.partial(_gla_kernel, reverse=reverse, nblk=nblk),
        grid=(b, nblk),
        in_specs=[row(128), row(128), row(128, direction), row(256), st],
        out_specs=[row(256), st],
        out_shape=[jax.ShapeDtypeStruct((b, l, GLA_WIDTH), F32), jax.ShapeDtypeStruct((b, GLA_WIDTH, 128), F32)],
        scratch_shapes=[pltpu.VMEM((GLA_WIDTH, 128), F32)],
        compiler_params=_cparams(("arbitrary", "arbitrary")),
    )(p['gq'], p['gk'], p['la'], p['gv'], s0)


def _attn_kernel(qt_ref, k_ref, vt_ref, o_ref, m_scr, acc_scr, *, nk):
    ki = pl.program_id(3)

    @pl.when(ki == 0)
    def _():
        m_scr[...] = jnp.full(m_scr.shape, -jnp.inf, F32)
        acc_scr[...] = jnp.zeros(acc_scr.shape, F32)

    qt = qt_ref[0, 0]
    m, acc = m_scr[...], acc_scr[...]
    nsub = k_ref.shape[2] // ATT_SUB

    def scores(c):
        return _dot(k_ref[0, 0, c * ATT_SUB:(c + 1) * ATT_SUB, :], qt)

    def weighted_values(acc_in, alpha_c, p_c, c):
        return alpha_c * acc_in + _dot(vt_ref[0, 0, :, c * ATT_SUB:(c + 1) * ATT_SUB], p_c)

    ahead = [scores(c) for c in range(min(ATT_AHEAD, nsub))]
    pending = None
    for c in range(nsub):
        s = ahead.pop(0)
        if c + ATT_AHEAD < nsub:
            ahead.append(scores(c + ATT_AHEAD))
        m_new = jnp.maximum(m, jnp.max(s, axis=0, keepdims=True))
        alpha = jnp.exp2(m - m_new)
        p = jnp.exp2(s - m_new).astype(BF16)
        m = m_new
        if pending is not None:
            acc = weighted_values(acc, *pending)
        pending = (alpha, p, c)
    acc = weighted_values(acc, *pending)
    m_scr[...], acc_scr[...] = m, acc

    @pl.when(ki == nk - 1)
    def _():
        fin = acc_scr[...]
        o_ref[0, 0] = (fin[0:ATT_DV, :] / fin[ATT_DV:ATT_DV + 1, :]).astype(o_ref.dtype)


def _pick_tile(n, options):
    for t in options:
        if n % t == 0:
            return t
    raise ValueError(f"no tile for {n}")


def _attn_call(qt, k, vt, group):
    b, nh, dk, lq = qt.shape
    lk = k.shape[2]
    tq = _pick_tile(lq, (1024, 512, 256))
    tk = _pick_tile(lk, (8448, 2816, 768, 512, 256))
    nk = lk // tk

    def kvh(h):
        return h // group

    return pl.pallas_call(
        functools.partial(_attn_kernel, nk=nk),
        grid=(b, nh, lq // tq, nk),
        in_specs=[pl.BlockSpec((1, 1, dk, tq), lambda bi, h, qi, ki: (bi, h, 0, qi)),
                  pl.BlockSpec((1, 1, tk, dk), lambda bi, h, qi, ki: (bi, kvh(h), ki, 0)),
                  pl.BlockSpec((1, 1, ATT_DVP, tk), lambda bi, h, qi, ki: (bi, kvh(h), 0, ki))],
        out_specs=pl.BlockSpec((1, 1, ATT_DV, tq), lambda bi, h, qi, ki: (bi, h, 0, qi)),
        out_shape=jax.ShapeDtypeStruct((b, nh, ATT_DV, lq), BF16),
        scratch_shapes=[pltpu.VMEM((1, tq), F32), pltpu.VMEM((ATT_DVP, tq), F32)],
        compiler_params=_cparams(("arbitrary", "arbitrary", "arbitrary", "arbitrary")),
    )(qt, k, vt)


def _merge_kernel(x_ref, of_ref, ob_ref, gr_ref, oa_ref, g1_ref, sc_ref, sh_ref, ggla_ref, bs_ref, wout_ref,
                  lng_ref, lnb_ref, wrh_ref, wrl_ref, x1_ref, h2_ref, aff_ref):
    o = of_ref[0] + ob_ref[0]
    hi, lo = _split2(o * o)
    ms = (_dot(hi, bs_ref[...]) + _dot(lo, bs_ref[...])) * (1.0 / GLA_DV)
    r = gr_ref[0]
    gla = o * lax.rsqrt(ms + RMS_EPS) * ggla_ref[...] * (r * (1.0 / (1.0 + jnp.exp(-r))))
    a = _dot(gla.astype(BF16), wout_ref[0:GLA_WIDTH, :]) + _dot(oa_ref[0], wout_ref[GLA_WIDTH:, :])
    x1 = _layer_norm(DEEPNORM_ALPHA * x_ref[0] + g1_ref[0] * a, lng_ref[...], lnb_ref[...])
    x1_ref[0] = x1
    h2 = x1 * (1.0 + sc_ref[0]) + sh_ref[0]
    h2h, h2l = _split2(h2)
    h2_ref[0] = h2h
    wh, wl = wrh_ref[...], wrl_ref[...]
    logits = _dot_nt(wh, h2h) + _dot_nt(wh, h2l) + _dot_nt(wl, h2h)
    e = jnp.exp(logits - jnp.max(logits, axis=0, keepdims=True))
    aff_ref[0] = e / jnp.sum(e, axis=0, keepdims=True)


def _merge_call(x, of, ob, gr, oatt, g1, sc2, sh2, w):
    b, l, d = x.shape
    t = min(l, 256)
    full = lambda arr: pl.BlockSpec(arr.shape, lambda bi, i: (0,) * arr.ndim)
    row = lambda n: pl.BlockSpec((1, t, n), lambda bi, i: (bi, i, 0))
    vec = pl.BlockSpec((1, 1, d), lambda bi, i: (bi, 0, 0))
    consts = [w['g_gla'], w['bsum64'], w['w_out'], w['ln1_g'], w['ln1_b'], w['wr_hi'], w['wr_lo']]
    return pl.pallas_call(
        _merge_kernel,
        grid=(b, l // t),
        in_specs=[row(d), row(256), row(256), row(256), row(768), vec, vec, vec] + [full(a) for a in consts],
        out_specs=[row(d), row(d), pl.BlockSpec((1, N_EXPERTS, t), lambda bi, i: (bi, 0, i))],
        out_shape=[jax.ShapeDtypeStruct((b, l, d), F32), jax.ShapeDtypeStruct((b, l, d), BF16),
                   jax.ShapeDtypeStruct((b, N_EXPERTS, l), F32)],
        compiler_params=_cparams(("arbitrary", "arbitrary")),
    )(x, of, ob, gr, oatt, g1, sc2, sh2, *consts)


def _topk_kernel(aff_ref, slot_ref, off_ref, *, nc, cap):
    shift = int(math.log2(nc))
    bits = pltpu.bitcast(aff_ref[0], jnp.int32)
    r = bits.shape[0]
    ri = lax.broadcasted_iota(jnp.int32, (r, r), 0)
    ci = lax.broadcasted_iota(jnp.int32, (r, r), 1)
    same = (ri >> shift) == (ci >> shift)
    ones_bd = jnp.where(same, 1.0, 0.0).astype(BF16)
    lstrict = jnp.where(jnp.logical_and(same, ci < ri), 1.0, 0.0).astype(BF16)
    li = lax.broadcasted_iota(jnp.int32, (LANE, LANE), 0)
    lj = lax.broadcasted_iota(jnp.int32, (LANE, LANE), 1)
    uex = jnp.where(li < lj, 1.0, 0.0).astype(BF16)
    ones = jnp.ones((LANE, LANE), BF16)

    def total(mask):
        colsum = _dot(ones_bd, jnp.where(mask, 1.0, 0.0).astype(BF16))
        return _dot(colsum.astype(BF16), ones)

    def excl_cumsum(mask):
        mb = jnp.where(mask, 1.0, 0.0).astype(BF16)
        off = _dot(lstrict, _dot(mb, ones).astype(BF16))
        return _dot(mb, uex) + off, off

    def body(it, v):
        cand = v | lax.shift_left(jnp.int32(1), jnp.int32(30) - it)
        return jnp.where(total(bits >= cand) >= cap, cand, v)

    v = lax.fori_loop(0, 31, body, jnp.zeros(bits.shape, jnp.int32))
    gt = bits > v
    eq = bits == v
    need = cap - total(gt)
    rank_eq, _ = excl_cumsum(eq)
    sel = jnp.logical_or(gt, jnp.logical_and(eq, rank_eq < need))
    pos, off = excl_cumsum(sel)
    slot_ref[0] = jnp.where(sel, pos.astype(jnp.int32), -1)
    off_ref[0] = off.astype(jnp.int32)


def _topk_call(aff):
    b, e, l = aff.shape
    nc = l // LANE
    cap = EC_CAPACITY_FACTOR * l // N_EXPERTS
    r = e * nc
    spec = pl.BlockSpec((1, r, LANE), lambda bi: (bi, 0, 0))
    return pl.pallas_call(
        functools.partial(_topk_kernel, nc=nc, cap=cap),
        grid=(b,),
        in_specs=[spec],
        out_specs=[spec, spec],
        out_shape=[jax.ShapeDtypeStruct((b, r, LANE), jnp.int32)] * 2,
        compiler_params=_cparams(("arbitrary",)),
    )(aff.reshape(b, r, LANE))


def _expert_kernel(s0_ref, h_ref, slot_ref, aff_ref, wg_ref, wu_ref, wd_ref, ohi_ref, olo_ref, xs_scr, gate_scr,
                   *, nb, nsub, nc, cap, win):
    e, b, j = pl.program_id(0), pl.program_id(1), pl.program_id(2)

    @pl.when(j == 0)
    def _():
        xs_scr[...] = jnp.zeros(xs_scr.shape, F32)
        gate_scr[...] = jnp.zeros(gate_scr.shape, F32)

    base = (b * N_EXPERTS + e) * nc + j * nsub
    rows = lax.broadcasted_iota(jnp.int32, (win, LANE), 0)
    for u in range(nsub):
        s0a = pl.multiple_of((s0_ref[base + u] >> 3) << 3, 8)
        lanes = slice(u * LANE, (u + 1) * LANE)
        hit = (rows + s0a) == slot_ref[0, :, lanes]
        comp = _dot(jnp.where(hit, 1.0, 0.0).astype(BF16), h_ref[0, lanes, :])
        xs_scr[pl.ds(s0a, win), :] += comp
        gate_scr[pl.ds(s0a, win), :] += jnp.sum(jnp.where(hit, aff_ref[0, :, lanes], 0.0), axis=1, keepdims=True)

    @pl.when(j == nb - 1)
    def _():
        rc = min(cap, 256)
        for c0 in range(0, cap, rc):
            xs = xs_scr[c0:c0 + rc, :].astype(BF16)
            gg = _dot(xs, wg_ref[0])
            hid = (gg * (1.0 / (1.0 + jnp.exp(-gg)))) * _dot(xs, wu_ref[0])
            out = _dot(hid.astype(BF16), wd_ref[0]) * gate_scr[c0:c0 + rc, :]
            hi, lo = _split2(out)
            ohi_ref[0, 0, c0:c0 + rc, :] = hi
            olo_ref[0, 0, c0:c0 + rc, :] = lo
        pad = jnp.zeros((ohi_ref.shape[2] - cap, ohi_ref.shape[3]), BF16)
        ohi_ref[0, 0, cap:, :] = pad
        olo_ref[0, 0, cap:, :] = pad


def _expert_call(h2, slot, aff, s0, wg, wu, wd):
    b, l, d = h2.shape
    nc = l // LANE
    cap = EC_CAPACITY_FACTOR * l // N_EXPERTS
    tb = min(l, 2048)
    nsub = tb // LANE
    nb = l // tb
    win = LANE + 8
    f = wg.shape[2]
    row3 = lambda e, bi, j, s: ((bi * N_EXPERTS + e) * nb + j, 0, 0)
    grid_spec = pltpu.PrefetchScalarGridSpec(
        num_scalar_prefetch=1,
        grid=(N_EXPERTS, b, nb),
        in_specs=[pl.BlockSpec((1, tb, d), lambda e, bi, j, s: (bi, j, 0)),
                  pl.BlockSpec((1, 1, tb), row3),
                  pl.BlockSpec((1, 1, tb), row3),
                  pl.BlockSpec((1, d, f), lambda e, bi, j, s: (e, 0, 0)),
                  pl.BlockSpec((1, d, f), lambda e, bi, j, s: (e, 0, 0)),
                  pl.BlockSpec((1, f, d), lambda e, bi, j, s: (e, 0, 0))],
        out_specs=[pl.BlockSpec((1, 1, cap + COMBINE_WIN, d), lambda e, bi, j, s: (bi, e, 0, 0))] * 2,
        scratch_shapes=[pltpu.VMEM((cap + win, d), F32), pltpu.VMEM((cap + win, 1), F32)],
    )
    return pl.pallas_call(
        functools.partial(_expert_kernel, nb=nb, nsub=nsub, nc=nc, cap=cap, win=win),
        grid_spec=grid_spec,
        out_shape=[jax.ShapeDtypeStruct((b, N_EXPERTS, cap + COMBINE_WIN, d), BF16)] * 2,
        compiler_params=_cparams(("arbitrary", "arbitrary", "arbitrary")),
    )(s0, h2, slot.reshape(b * N_EXPERTS * nb, 1, tb), aff.reshape(b * N_EXPERTS * nb, 1, tb), wg, wu, wd)


COMBINE_WIN = LANE + 16


COMBINE_BASE = 48
COMBINE_EXT = COMBINE_WIN - COMBINE_BASE


def _combine_kernel(s0_ref, end_ref, ohi_hbm, olo_hbm, slot_ref, x1_ref, g2_ref, lng_ref, lnb_ref, o_ref,
                    hbase, lbase, hext, lext, sem, *, nc, nsteps):
    b, j = pl.program_id(0), pl.program_id(1)
    t = b * nc + j
    cur = t % 2

    def win_start(bb, jj, e):
        return pl.multiple_of((s0_ref[(bb * N_EXPERTS + e) * nc + jj] >> 4) << 4, 16)

    def needs_ext(bb, jj, e):
        return end_ref[(bb * N_EXPERTS + e) * nc + jj] > win_start(bb, jj, e) + COMBINE_BASE

    def base_copies(bb, jj, sl):
        out = []
        for e in range(N_EXPERTS):
            rows = pl.ds(win_start(bb, jj, e), COMBINE_BASE)
            dst = pl.ds(e * COMBINE_BASE, COMBINE_BASE)
            out.append(pltpu.make_async_copy(ohi_hbm.at[bb, e, rows, :], hbase.at[sl, dst, :], sem.at[sl, 0]))
            out.append(pltpu.make_async_copy(olo_hbm.at[bb, e, rows, :], lbase.at[sl, dst, :], sem.at[sl, 1]))
        return out

    def ext_copies(bb, jj, sl, e):
        rows = pl.ds(win_start(bb, jj, e) + COMBINE_BASE, COMBINE_EXT)
        dst = pl.ds(e * COMBINE_EXT, COMBINE_EXT)
        return [pltpu.make_async_copy(ohi_hbm.at[bb, e, rows, :], hext.at[sl, dst, :], sem.at[sl, 2]),
                pltpu.make_async_copy(olo_hbm.at[bb, e, rows, :], lext.at[sl, dst, :], sem.at[sl, 3])]

    def start_all(bb, jj, sl):
        for cp in base_copies(bb, jj, sl):
            cp.start()
        for e in range(N_EXPERTS):
            @pl.when(needs_ext(bb, jj, e))
            def _():
                for cp in ext_copies(bb, jj, sl, e):
                    cp.start()

    @pl.when(t == 0)
    def _():
        hext[...] = jnp.zeros(hext.shape, BF16)
        lext[...] = jnp.zeros(lext.shape, BF16)
        start_all(b, j, cur)

    @pl.when(t + 1 < nsteps)
    def _():
        wrap = j + 1 == nc
        start_all(jnp.where(wrap, b + 1, b), jnp.where(wrap, 0, j + 1), 1 - cur)

    for cp in base_copies(b, j, cur):
        cp.wait()
    any_ext = needs_ext(b, j, 0)
    for e in range(N_EXPERTS):
        any_ext = jnp.logical_or(any_ext, needs_ext(b, j, e))

        @pl.when(needs_ext(b, j, e))
        def _():
            for cp in ext_copies(b, j, cur, e):
                cp.wait()

    def onehot(nrows, offset):
        rows = lax.broadcasted_iota(jnp.int32, (nrows, LANE), 0) + offset
        return jnp.concatenate(
            [jnp.where((rows + win_start(b, j, e)) == slot_ref[0, e:e + 1, :], 1.0, 0.0).astype(BF16)
             for e in range(N_EXPERTS)], axis=0)

    p = onehot(COMBINE_BASE, 0)
    z = DEEPNORM_ALPHA * x1_ref[0] + g2_ref[0] * (_dot_tn(p, hbase[cur]) + _dot_tn(p, lbase[cur]))
    o_ref[0] = _layer_norm(z, lng_ref[...], lnb_ref[...])

    @pl.when(any_ext)
    def _():
        pe = onehot(COMBINE_EXT, COMBINE_BASE)
        z2 = z + g2_ref[0] * (_dot_tn(pe, hext[cur]) + _dot_tn(pe, lext[cur]))
        o_ref[0] = _layer_norm(z2, lng_ref[...], lnb_ref[...])


def _combine_call(x1, ohi, olo, slot, s0, end, g2, lng, lnb):
    b, l, d = x1.shape
    nc = l // LANE
    full = lambda arr: pl.BlockSpec(arr.shape, lambda bi, j, s, en: (0,) * arr.ndim)
    grid_spec = pltpu.PrefetchScalarGridSpec(
        num_scalar_prefetch=2,
        grid=(b, nc),
        in_specs=[pl.BlockSpec(memory_space=pl.ANY), pl.BlockSpec(memory_space=pl.ANY),
                  pl.BlockSpec((1, N_EXPERTS, LANE), lambda bi, j, s, en: (bi, 0, j)),
                  pl.BlockSpec((1, LANE, d), lambda bi, j, s, en: (bi, j, 0)),
                  pl.BlockSpec((1, 1, d), lambda bi, j, s, en: (bi, 0, 0)),
                  full(lng), full(lnb)],
        out_specs=pl.BlockSpec((1, LANE, d), lambda bi, j, s, en: (bi, j, 0)),
        scratch_shapes=[pltpu.VMEM((2, N_EXPERTS * COMBINE_BASE, d), BF16),
                        pltpu.VMEM((2, N_EXPERTS * COMBINE_BASE, d), BF16),
                        pltpu.VMEM((2, N_EXPERTS * COMBINE_EXT, d), BF16),
                        pltpu.VMEM((2, N_EXPERTS * COMBINE_EXT, d), BF16),
                        pltpu.SemaphoreType.DMA((2, 4))],
    )
    return pl.pallas_call(
        functools.partial(_combine_kernel, nc=nc, nsteps=b * nc),
        grid_spec=grid_spec,
        out_shape=jax.ShapeDtypeStruct((b, l, d), F32),
        compiler_params=_cparams(("arbitrary", "arbitrary")),
    )(s0, end, ohi, olo, slot.reshape(b, N_EXPERTS, l), x1, g2, lng, lnb)


def _swap_halves(w, nheads, dh):
    q = dh // 4
    return jnp.flip(w.reshape(w.shape[0], nheads, 2, 2, q), axis=3).reshape(w.shape[0], nheads * dh)


def _prep_layer(l, w_in, w_gla_gate_f, b_gla_gate_f, w_gla_gate_b, b_gla_gate_b, g_gla_norm, g_mla_qa, w_mla_qb,
                g_mla_kva, w_mla_kvb, g_gqa_q, g_gqa_k, w_out, ln1_g, ln1_b, w_router, ln2_g, ln2_b):
    wi = w_in[l]
    d = wi.shape[0]
    o = IN_OFF
    aq, ak, av = wi[:, o[9]:o[10]], wi[:, o[10]:o[11]], wi[:, o[11]:o[12]]
    mkr = wi[:, o[8]:o[9]]
    zeros = lambda n: jnp.zeros((d, n), F32)
    g_blk = jnp.concatenate([wi[:, o[4]:o[6]], zeros(32), mkr, zeros(32)], axis=1)
    gs_blk = jnp.concatenate([zeros(64), _swap_halves(mkr, 1, MLA_ROPE), zeros(32)], axis=1)
    w_in_r = jnp.concatenate([
        wi[:, 0:o[4]], wi[:, o[6]:o[7]], wi[:, o[7]:o[8]],
        aq, ak, av, g_blk,
        _swap_halves(aq, GQA_HEADS, GQA_DH), _swap_halves(ak, GQA_KV_HEADS, GQA_DH), gs_blk], axis=1).astype(BF16)
    assert w_in_r.shape[1] == C_END

    r = GLA_GATE_RANK
    wg = jnp.zeros((LANE, 256), F32)
    wg = wg.at[0:r, 0:128].set(w_gla_gate_f[l]).at[r:2 * r, 128:256].set(w_gla_gate_b[l]).astype(BF16)
    bg = jnp.concatenate([b_gla_gate_f[l], b_gla_gate_b[l]])[None, :]

    qb = w_mla_qb[l].reshape(MLA_Q_RANK, MLA_HEADS, MLA_NOPE + MLA_ROPE)
    qb_rope_sw = _swap_halves(qb[:, :, MLA_NOPE:].reshape(MLA_Q_RANK, -1), MLA_HEADS, MLA_ROPE).reshape(
        MLA_Q_RANK, MLA_HEADS, MLA_ROPE)
    qb_sw = jnp.concatenate([jnp.zeros_like(qb[:, :, :MLA_NOPE]), qb_rope_sw], axis=2)
    padq = lambda t: jnp.pad(t, ((0, 0), (0, 0), (0, HEAD_PAD - t.shape[2]))).reshape(MLA_Q_RANK, -1)
    wqb = jnp.concatenate([padq(qb), padq(qb_sw)], axis=1).astype(BF16)

    kvb = w_mla_kvb[l].reshape(MLA_KV_RANK, MLA_HEADS, MLA_NOPE + MLA_DV)
    kpart = jnp.pad(kvb[:, :, :MLA_NOPE], ((0, 0), (0, 0), (0, HEAD_PAD - MLA_NOPE))).reshape(MLA_KV_RANK, -1)
    wkvb = jnp.concatenate([kpart, kvb[:, :, MLA_NOPE:].reshape(MLA_KV_RANK, -1)], axis=1).astype(BF16)

    def gain_rows(gv, nheads):
        plain = jnp.tile(gv[None, :], (1, nheads))
        swapped = _swap_halves(plain, nheads, GQA_DH)
        width = GQA_HEADS * GQA_DH
        return [jnp.pad(t, ((0, 0), (0, width - t.shape[1]))) for t in (plain, swapped)]

    gains = jnp.concatenate(gain_rows(g_gqa_q[l], GQA_HEADS) + gain_rows(g_gqa_k[l], GQA_KV_HEADS)
                            + [jnp.zeros((4, GQA_HEADS * GQA_DH), F32)], axis=0)
    jj = np.arange(GLA_WIDTH)
    bsum64 = jnp.asarray((jj[:, None] // GLA_DV) == (jj[None, :] // GLA_DV), BF16)
    wr = w_router[l].T
    wr_hi = wr.astype(BF16)
    wr_lo = (wr - wr_hi.astype(F32)).astype(BF16)
    return dict(w_in=w_in_r, wg=wg, bg=bg, g_qa=g_mla_qa[l][None, :], wqb=wqb, g_kva=g_mla_kva[l][None, :],
                wkvb=wkvb, gains=gains, g_gla=jnp.tile(g_gla_norm[l][None, :], (1, GLA_HEADS)),
                bsum64=bsum64, w_out=w_out[l].astype(BF16), ln1_g=ln1_g[l][None, :], ln1_b=ln1_b[l][None, :],
                wr_hi=wr_hi, wr_lo=wr_lo, ln2_g=ln2_g[l][None, :], ln2_b=ln2_b[l][None, :])


def _tables(l, with_pos):
    sm = MLA_SCALE * LOG2E
    sg = GQA_SCALE * LOG2E
    z = lambda n: jnp.zeros((l, n), F32)
    one = lambda n: jnp.ones((l, n), F32)
    if with_pos:
        t = jnp.arange(l, dtype=jnp.int32)
        row = (t // GRID_W).astype(F32)[:, None]
        col = (t % GRID_W).astype(F32)[:, None]

        def cs(nfreq):
            inv = ROPE_THETA ** (-jnp.arange(nfreq, dtype=F32) / nfreq)
            ar, ac = row * inv, col * inv
            c = jnp.concatenate([jnp.cos(ar), jnp.cos(ar), jnp.cos(ac), jnp.cos(ac)], axis=1)
            s = jnp.concatenate([-jnp.sin(ar), jnp.sin(ar), -jnp.sin(ac), jnp.sin(ac)], axis=1)
            return c, s

        c32, s32 = cs(MLA_ROPE // 4)
        c64, s64 = cs(GQA_DH // 4)
    else:
        c32, s32, c64, s64 = one(32), z(32), one(64), z(64)
    cm = jnp.concatenate([one(64), c32, z(32)], axis=1) * sm
    smm = jnp.concatenate([z(64), s32, z(32)], axis=1) * sm
    ck = jnp.concatenate([z(64), c32, z(32)], axis=1)
    sk = jnp.concatenate([z(64), s32, z(32)], axis=1)
    cg = jnp.concatenate([c64, c64], axis=1)
    sgg = jnp.concatenate([s64, s64], axis=1)
    return jnp.concatenate([cm, smm, ck, sk, cg * sg, sgg * sg, cg, sgg], axis=1)


def _heads_t(a, nheads, dh):
    b, l, _ = a.shape
    return a.reshape(b, l, nheads, dh).transpose(0, 2, 3, 1)


def _heads(a, nheads, dh):
    b, l, _ = a.shape
    return a.reshape(b, l, nheads, dh).transpose(0, 2, 1, 3)


def _attend(q, qg, k, kg, v):
    b, lq, _ = q.shape
    vt = _heads_t(v, N_KVHEADS, ATT_DV)
    extra = jnp.zeros(vt.shape[:2] + (ATT_DVP - ATT_DV, vt.shape[3]), BF16).at[:, :, 0, :].set(1.0)
    vt = jnp.concatenate([vt, extra], axis=2)
    ot_mla = _attn_call(_heads_t(q, MLA_HEADS, HEAD_PAD), _heads(k, MLA_HEADS, HEAD_PAD), vt[:, :MLA_HEADS], 1)
    ot_gqa = _attn_call(_heads_t(qg, GQA_HEADS, GQA_DH), _heads(kg, GQA_KV_HEADS, GQA_DH), vt[:, MLA_HEADS:],
                        GQA_GROUP)
    ot = jnp.concatenate([ot_mla, ot_gqa], axis=1)
    return ot.transpose(0, 3, 1, 2).reshape(b, lq, N_QHEADS * ATT_DV)


def _moe(h2, aff, x1, g2, w, wexp):
    b, l, _ = h2.shape
    nc = l // LANE
    cap = EC_CAPACITY_FACTOR * l // N_EXPERTS
    slot, off = _topk_call(aff)
    s0 = off[:, :, 0].reshape(-1)
    ohi, olo = _expert_call(h2, slot, aff, s0, *wexp)
    starts = s0.reshape(b, N_EXPERTS, nc)
    end = jnp.concatenate([starts[:, :, 1:], jnp.full((b, N_EXPERTS, 1), cap, jnp.int32)], axis=2).reshape(-1)
    return _combine_call(x1, ohi, olo, slot, s0, end, g2, w['ln2_g'], w['ln2_b'])


def kernel(x, c, ctx, c_ctx, w_mod, b_mod, w_in, w_gla_gate_f, b_gla_gate_f, w_gla_gate_b, b_gla_gate_b, g_gla_norm, g_mla_qa, w_mla_qb, g_mla_kva, w_mla_kvb, g_gqa_q, g_gqa_k, w_out, ln1_g, ln1_b, w_router, w_exp_gate, w_exp_up, w_exp_down, ln2_g, ln2_b):
    b, l, d = x.shape
    lc = ctx.shape[1]
    assert b < 8 and l % 256 == 0 and lc % 256 == 0
    cvec = jnp.zeros((8, d), F32).at[0:b].set(c).at[b].set(c_ctx)
    mods = _mod_call(cvec, w_mod, b_mod)
    tab_l = _tables(l, True)
    tab_c = _tables(lc, False)
    zero_state = jnp.zeros((b, GLA_WIDTH, 128), F32)
    cx = ctx
    for li in range(DEPTH):
        need_ctx = li < DEPTH - 1
        w = _prep_layer(li, w_in, w_gla_gate_f, b_gla_gate_f, w_gla_gate_b, b_gla_gate_b, g_gla_norm, g_mla_qa,
                        w_mla_qb, g_mla_kva, w_mla_kvb, g_gqa_q, g_gqa_k, w_out, ln1_g, ln1_b, w_router, ln2_g,
                        ln2_b)
        wexp = (w_exp_gate[li].astype(BF16), w_exp_up[li].astype(BF16), w_exp_down[li].astype(BF16))
        sh1, sc1, g1, sh2, sc2, g2 = [m[:, None, :] for m in jnp.split(mods[li, 0:b], 6, axis=-1)]
        csh1, csc1, cg1, csh2, csc2, cg2 = [jnp.broadcast_to(m[:, None, :], (b, 1, d))
                                            for m in jnp.split(mods[li, b:b + 1], 6, axis=-1)]
        names = ('gq', 'gk', 'la', 'gv', 'gr', 'q', 'qg', 'k', 'kg', 'v')
        pl_ = dict(zip(names, _proj_call(x, sc1, sh1, tab_l, w)))
        pc_ = dict(zip(names, _proj_call(cx, csc1, csh1, tab_c, w)))
        ocf, s_f = _gla_call(pc_, zero_state, 0)
        ocb, s_b = _gla_call(pc_, zero_state, 1)
        olf, _ = _gla_call(pl_, s_f, 0)
        olb, _ = _gla_call(pl_, s_b, 1)
        both = lambda name: jnp.concatenate([pc_[name], pl_[name]], axis=1)
        oatt = _attend(pl_['q'], pl_['qg'], both('k'), both('kg'), both('v'))
        x1, h2, aff = _merge_call(x, olf, olb, pl_['gr'], oatt, g1, sc2, sh2, w)
        x = _moe(h2, aff, x1, g2, w, wexp)
        if need_ctx:
            oatt_c = _attend(pc_['q'], pc_['qg'], pc_['k'], pc_['kg'], pc_['v'])
            c1, ch2, caff = _merge_call(cx, ocf, ocb, pc_['gr'], oatt_c, cg1, csc2, csh2, w)
            cx = _moe(ch2, caff, c1, cg2, w, wexp)
    return x
```

```python
import functools
import math

import numpy as np
import jax
import jax.numpy as jnp
from jax import lax
from jax.experimental import pallas as pl
from jax.experimental.pallas import tpu as pltpu

F32 = jnp.float32
BF16 = jnp.bfloat16

D_MODEL = 1024
DEPTH = 4
GRID_W = 64
ROPE_THETA = 10000.0
LN_EPS = 1e-6
RMS_EPS = 1e-6
DEEPNORM_ALPHA = (2 * DEPTH) ** 0.25

GLA_HEADS, GLA_DK, GLA_DV, GLA_GATE_RANK, GLA_TAU, GLA_CHUNK = 4, 32, 64, 16, 16.0, 64
MLA_HEADS, MLA_Q_RANK, MLA_KV_RANK, MLA_NOPE, MLA_ROPE, MLA_DV = 6, 384, 256, 64, 32, 64
MLA_SCALE = (MLA_NOPE + MLA_ROPE) ** -0.5
GQA_HEADS, GQA_KV_HEADS, GQA_DH = 6, 2, 64
GQA_GROUP = GQA_HEADS // GQA_KV_HEADS
GQA_SCALE = GQA_DH ** -0.5
N_EXPERTS = 16
EC_CAPACITY_FACTOR = 2
D_EXPERT = 1024

GLA_WIDTH = GLA_HEADS * GLA_DV
IN_SPLITS = (GLA_HEADS * GLA_DK, GLA_HEADS * GLA_DK, GLA_WIDTH, GLA_WIDTH, GLA_GATE_RANK, GLA_GATE_RANK,
             MLA_Q_RANK, MLA_KV_RANK, MLA_ROPE,
             GQA_HEADS * GQA_DH, GQA_KV_HEADS * GQA_DH, GQA_KV_HEADS * GQA_DH)
IN_OFF = [0] + [int(o) for o in np.cumsum(IN_SPLITS)]

LANE = 128
HEAD_PAD = 128
N_QHEADS = MLA_HEADS + GQA_HEADS
N_KVHEADS = MLA_HEADS + GQA_KV_HEADS
ATT_DV = 64
ATT_DVP = 80
ATT_SUB = 128
ATT_AHEAD = 2
LOG2E = math.log2(math.e)
VMEM_LIMIT = 56 * 1024 * 1024

C_A, C_MQA, C_MKVA, C_AQ, C_AK, C_AV, C_G, C_AQS, C_AKS, C_GS, C_END = (
    0, 768, 1152, 1408, 1792, 1920, 2048, 2176, 2560, 2688, 2816)


def _cparams(sem):
    return pltpu.CompilerParams(dimension_semantics=sem, vmem_limit_bytes=VMEM_LIMIT)


def _dot(a, b):
    return jnp.dot(a, b, preferred_element_type=F32)


def _dot_nt(a, b):
    return lax.dot_general(a, b, (((1,), (1,)), ((), ())), preferred_element_type=F32)


def _dot_tn(a, b):
    return lax.dot_general(a, b, (((0,), (0,)), ((), ())), preferred_element_type=F32)


def _split2(x):
    hi = x.astype(BF16)
    lo = (x - hi.astype(F32)).astype(BF16)
    return hi, lo


def _split3(x):
    hi = x.astype(BF16)
    r = x - hi.astype(F32)
    mid = r.astype(BF16)
    lo = (r - mid.astype(F32)).astype(BF16)
    return hi, mid, lo


def _layer_norm(z, g, b):
    mu = jnp.mean(z, axis=-1, keepdims=True)
    zc = z - mu
    var = jnp.mean(zc * zc, axis=-1, keepdims=True)
    return zc * lax.rsqrt(var + LN_EPS) * g + b


def _mod_kernel(c_ref, w_ref, b_ref, o_ref):
    cv = c_ref[...]
    s = cv * (1.0 / (1.0 + jnp.exp(-cv)))
    o_ref[0] = _dot(s.astype(BF16), w_ref[0].astype(BF16)) + b_ref[0]


def _mod_call(cvec, w_mod, b_mod):
    depth, d, n6 = w_mod.shape
    bn = 768
    return pl.pallas_call(
        _mod_kernel,
        grid=(depth, n6 // bn),
        in_specs=[pl.BlockSpec((8, d), lambda l, j: (0, 0)),
                  pl.BlockSpec((1, d, bn), lambda l, j: (l, 0, j)),
                  pl.BlockSpec((1, 1, bn), lambda l, j: (l, 0, j))],
        out_specs=pl.BlockSpec((1, 8, bn), lambda l, j: (l, 0, j)),
        out_shape=jax.ShapeDtypeStruct((depth, 8, n6), F32),
        compiler_params=_cparams(("arbitrary", "arbitrary")),
    )(cvec, w_mod, b_mod.reshape(depth, 1, n6))


def _proj_kernel(x_ref, sc_ref, sh_ref, tab_ref, w_ref, wg_ref, bg_ref, gqa_ref, wqb_ref, gkva_ref, wkvb_ref,
                 gains_ref,
                 gq_ref, gk_ref, la_ref, gv_ref, gr_ref, q_ref, qg_ref, k_ref, kg_ref, v_ref):
    h = x_ref[0] * (1.0 + sc_ref[0]) + sh_ref[0]
    hb = h.astype(BF16)

    def mm(a, b):
        return _dot(hb, w_ref[:, a:b])

    def tab(i):
        return tab_ref[:, i * LANE:(i + 1) * LANE]

    a = mm(C_A, C_MQA)
    gq_ref[0] = a[:, 0:128] * (GLA_DK ** -0.5)
    gk_ref[0] = a[:, 128:256]
    gv_ref[0] = a[:, 256:512]
    gr_ref[0] = a[:, 512:768]

    g = mm(C_G, C_AQS)
    gs = mm(C_GS, C_END)
    lg = _dot(g.astype(BF16), wg_ref[...]) + bg_ref[...]
    la_ref[0] = (jnp.minimum(lg, 0.0) - jnp.log1p(jnp.exp(-jnp.abs(lg)))) * (1.0 / GLA_TAU)

    mqa = mm(C_MQA, C_MKVA)
    nq = mqa * lax.rsqrt(jnp.mean(mqa * mqa, axis=-1, keepdims=True) + RMS_EPS) * gqa_ref[...]
    q2 = _dot(nq.astype(BF16), wqb_ref[...])
    cm, sm = tab(0), tab(1)
    for hh in range(MLA_HEADS):
        lo = hh * HEAD_PAD
        qh = q2[:, lo:lo + HEAD_PAD] * cm + q2[:, 768 + lo:768 + lo + HEAD_PAD] * sm
        q_ref[0, :, lo:lo + HEAD_PAD] = qh.astype(BF16)

    mkva = mm(C_MKVA, C_AQ)
    nkv = mkva * lax.rsqrt(jnp.mean(mkva * mkva, axis=-1, keepdims=True) + RMS_EPS) * gkva_ref[...]
    kv = _dot(nkv.astype(BF16), wkvb_ref[...])
    kr = g * tab(2) + gs * tab(3)
    for hh in range(MLA_HEADS):
        lo = hh * HEAD_PAD
        k_ref[0, :, lo:lo + HEAD_PAD] = (kv[:, lo:lo + HEAD_PAD] + kr).astype(BF16)
    v_ref[0, :, 0:384] = kv[:, 768:1152].astype(BF16)

    first = lax.broadcasted_iota(jnp.int32, (1, LANE), 1) < GQA_DH

    def pair_rs(t):
        sq = t * t
        s_lo = jnp.sum(jnp.where(first, sq, 0.0), axis=-1, keepdims=True)
        s_hi = jnp.sum(jnp.where(first, 0.0, sq), axis=-1, keepdims=True)
        return lax.rsqrt(jnp.where(first, s_lo, s_hi) * (1.0 / GQA_DH) + RMS_EPS)

    def normed_rotary(x, xs, g_row, gs_row, cos, sin):
        rs = pair_rs(x)
        return x * rs * g_row * cos + xs * rs * gs_row * sin

    aq = mm(C_AQ, C_AK)
    aqs = mm(C_AQS, C_AKS)
    for pp in range(GQA_HEADS // 2):
        sl = slice(pp * LANE, (pp + 1) * LANE)
        qh = normed_rotary(aq[:, sl], aqs[:, sl], gains_ref[0:1, sl], gains_ref[1:2, sl], tab(4), tab(5))
        qg_ref[0, :, sl] = qh.astype(BF16)
    kh = normed_rotary(mm(C_AK, C_AV), mm(C_AKS, C_GS), gains_ref[2:3, 0:LANE], gains_ref[3:4, 0:LANE],
                       tab(6), tab(7))
    kg_ref[0] = kh.astype(BF16)
    v_ref[0, :, 384:512] = mm(C_AV, C_G).astype(BF16)


def _proj_call(x, sc, sh, tab, w):
    b, l, d = x.shape
    t = min(l, 512)
    full = lambda arr: pl.BlockSpec(arr.shape, lambda bi, i: (0,) * arr.ndim)
    row = lambda n: pl.BlockSpec((1, t, n), lambda bi, i: (bi, i, 0))
    vec = pl.BlockSpec((1, 1, d), lambda bi, i: (bi, 0, 0))
    consts = [w['w_in'], w['wg'], w['bg'], w['g_qa'], w['wqb'], w['g_kva'], w['wkvb'], w['gains']]
    outs = [(128, F32), (128, F32), (256, F32), (256, F32), (256, F32),
            (768, BF16), (384, BF16), (768, BF16), (128, BF16), (512, BF16)]
    return pl.pallas_call(
        _proj_kernel,
        grid=(b, l // t),
        in_specs=[row(d), vec, vec, pl.BlockSpec((t, 8 * LANE), lambda bi, i: (i, 0))] + [full(a) for a in consts],
        out_specs=[row(n) for n, _ in outs],
        out_shape=[jax.ShapeDtypeStruct((b, l, n), dt) for n, dt in outs],
        compiler_params=_cparams(("arbitrary", "arbitrary")),
    )(x, sc, sh, tab, *consts)


GLA_BLOCK = 256


def _gla_kernel(q_ref, k_ref, la_ref, v_ref, s0_ref, o_ref, sfin_ref, s_scr, *, reverse, nblk):
    i = pl.program_id(1)

    @pl.when(i == 0)
    def _():
        s_scr[...] = s0_ref[0]

    r = q_ref.shape[1]
    nch = r // GLA_CHUNK
    la = la_ref[0]
    q = q_ref[0]
    k = k_ref[0]
    v = v_ref[0]
    ri = lax.broadcasted_iota(jnp.int32, (r, r), 0)
    ci = lax.broadcasted_iota(jnp.int32, (r, r), 1)
    same = (ri >> 6) == (ci >> 6)
    tri = jnp.logical_and(same, (ci >= ri) if reverse else (ci <= ri))
    trib = jnp.where(tri, 1.0, 0.0).astype(BF16)
    oneb = jnp.where(same, 1.0, 0.0).astype(BF16)
    hi, mid, lo = _split3(la)
    bcum = _dot(trib, hi) + _dot(trib, mid) + _dot(trib, lo)
    tot = _dot(oneb, hi) + _dot(oneb, mid) + _dot(oneb, lo)
    qd = q * jnp.exp(bcum)
    kinv = (k * jnp.exp(-bcum)).astype(BF16)
    kend = (k * jnp.exp(tot - bcum)).astype(BF16)
    dec = jnp.exp(tot)
    vb = v.astype(BF16)
    lane_k = lax.broadcasted_iota(jnp.int32, (1, GLA_HEADS * GLA_DK), 1)
    lane_v = lax.broadcasted_iota(jnp.int32, (1, GLA_WIDTH), 1)
    o = jnp.zeros((r, GLA_WIDTH), F32)
    for hh in range(GLA_HEADS):
        qh = jnp.where((lane_k >> 5) == hh, qd, 0.0).astype(BF16)
        att = jnp.where(tri, _dot_nt(qh, kinv), 0.0)
        vh = jnp.where((lane_v >> 6) == hh, v, 0.0).astype(BF16)
        o = o + _dot(att.astype(BF16), vh)
    qdb = qd.astype(BF16)
    bd = (lax.broadcasted_iota(jnp.int32, (GLA_WIDTH, GLA_HEADS * GLA_DK), 0) >> 6) == (
        lax.broadcasted_iota(jnp.int32, (GLA_WIDTH, GLA_HEADS * GLA_DK), 1) >> 5)
    s = s_scr[...]
    order = range(nch - 1, -1, -1) if reverse else range(nch)
    for cc in order:
        rows = slice(cc * GLA_CHUNK, (cc + 1) * GLA_CHUNK)
        o_ref[0, rows, :] = o[rows] + _dot_nt(qdb[rows], s.astype(BF16))
        ut = _dot_tn(vb[rows], kend[rows])
        s = dec[cc * GLA_CHUNK:cc * GLA_CHUNK + 1, :] * s + jnp.where(bd, ut, 0.0)
    s_scr[...] = s

    @pl.when(i == nblk - 1)
    def _():
        sfin_ref[0] = s


def _gla_call(p, s0, direction):
    b, l, _ = p['gq'].shape
    r = min(l, GLA_BLOCK)
    nblk = l // r
    reverse = direction == 1
    blk = (lambda i: nblk - 1 - i) if reverse else (lambda i: i)
    row = lambda n, col=0: pl.BlockSpec((1, r, n), lambda bi, i: (bi, blk(i), col))
    st = pl.BlockSpec((1, GLA_WIDTH, 128), lambda bi, i: (bi, 0, 0))
    return pl.pallas_call(
        functools.partial(_gla_kernel, reverse=reverse, nblk=nblk),
        grid=(b, nblk),
        in_specs=[row(128), row(128), row(128, direction), row(256), st],
        out_specs=[row(256), st],
        out_shape=[jax.ShapeDtypeStruct((b, l, GLA_WIDTH), F32), jax.ShapeDtypeStruct((b, GLA_WIDTH, 128), F32)],
        scratch_shapes=[pltpu.VMEM((GLA_WIDTH, 128), F32)],
        compiler_params=_cparams(("arbitrary", "arbitrary")),
    )(p['gq'], p['gk'], p['la'], p['gv'], s0)


def _attn_kernel(qt_ref, k_ref, vt_ref, o_ref, m_scr, acc_scr, *, nk):
    ki = pl.program_id(3)

    @pl.when(ki == 0)
    def _():
        m_scr[...] = jnp.full(m_scr.shape, -jnp.inf, F32)
        acc_scr[...] = jnp.zeros(acc_scr.shape, F32)

    qt = qt_ref[0, 0]
    m, acc = m_scr[...], acc_scr[...]
    nsub = k_ref.shape[2] // ATT_SUB

    def scores(c):
        return _dot(k_ref[0, 0, c * ATT_SUB:(c + 1) * ATT_SUB, :], qt)

    def weighted_values(acc_in, alpha_c, p_c, c):
        return alpha_c * acc_in + _dot(vt_ref[0, 0, :, c * ATT_SUB:(c + 1) * ATT_SUB], p_c)

    ahead = [scores(c) for c in range(min(ATT_AHEAD, nsub))]
    pending = None
    for c in range(nsub):
        s = ahead.pop(0)
        if c + ATT_AHEAD < nsub:
            ahead.append(scores(c + ATT_AHEAD))
        m_new = jnp.maximum(m, jnp.max(s, axis=0, keepdims=True))
        alpha = jnp.exp2(m - m_new)
        p = jnp.exp2(s - m_new).astype(BF16)
        m = m_new
        if pending is not None:
            acc = weighted_values(acc, *pending)
        pending = (alpha, p, c)
    acc = weighted_values(acc, *pending)
    m_scr[...], acc_scr[...] = m, acc

    @pl.when(ki == nk - 1)
    def _():
        fin = acc_scr[...]
        o_ref[0, 0] = (fin[0:ATT_DV, :] / fin[ATT_DV:ATT_DV + 1, :]).astype(o_ref.dtype)


def _pick_tile(n, options):
    for t in options:
        if n % t == 0:
            return t
    raise ValueError(f"no tile for {n}")


def _attn_call(qt, k, vt, group):
    b, nh, dk, lq = qt.shape
    lk = k.shape[2]
    tq = _pick_tile(lq, (1024, 512, 256))
    tk = _pick_tile(lk, (8448, 2816, 768, 512, 256))
    nk = lk // tk

    def kvh(h):
        return h // group

    return pl.pallas_call(
        functools.partial(_attn_kernel, nk=nk),
        grid=(b, nh, lq // tq, nk),
        in_specs=[pl.BlockSpec((1, 1, dk, tq), lambda bi, h, qi, ki: (bi, h, 0, qi)),
                  pl.BlockSpec((1, 1, tk, dk), lambda bi, h, qi, ki: (bi, kvh(h), ki, 0)),
                  pl.BlockSpec((1, 1, ATT_DVP, tk), lambda bi, h, qi, ki: (bi, kvh(h), 0, ki))],
        out_specs=pl.BlockSpec((1, 1, ATT_DV, tq), lambda bi, h, qi, ki: (bi, h, 0, qi)),
        out_shape=jax.ShapeDtypeStruct((b, nh, ATT_DV, lq), BF16),
        scratch_shapes=[pltpu.VMEM((1, tq), F32), pltpu.VMEM((ATT_DVP, tq), F32)],
        compiler_params=_cparams(("arbitrary", "arbitrary", "arbitrary", "arbitrary")),
    )(qt, k, vt)


def _merge_kernel(x_ref, of_ref, ob_ref, gr_ref, oa_ref, g1_ref, sc_ref, sh_ref, ggla_ref, bs_ref, wout_ref,
                  lng_ref, lnb_ref, wrh_ref, wrl_ref, x1_ref, h2_ref, aff_ref):
    o = of_ref[0] + ob_ref[0]
    hi, lo = _split2(o * o)
    ms = (_dot(hi, bs_ref[...]) + _dot(lo, bs_ref[...])) * (1.0 / GLA_DV)
    r = gr_ref[0]
    gla = o * lax.rsqrt(ms + RMS_EPS) * ggla_ref[...] * (r * (1.0 / (1.0 + jnp.exp(-r))))
    a = _dot(gla.astype(BF16), wout_ref[0:GLA_WIDTH, :]) + _dot(oa_ref[0], wout_ref[GLA_WIDTH:, :])
    x1 = _layer_norm(DEEPNORM_ALPHA * x_ref[0] + g1_ref[0] * a, lng_ref[...], lnb_ref[...])
    x1_ref[0] = x1
    h2 = x1 * (1.0 + sc_ref[0]) + sh_ref[0]
    h2h, h2l = _split2(h2)
    h2_ref[0] = h2h
    wh, wl = wrh_ref[...], wrl_ref[...]
    logits = _dot_nt(wh, h2h) + _dot_nt(wh, h2l) + _dot_nt(wl, h2h)
    e = jnp.exp(logits - jnp.max(logits, axis=0, keepdims=True))
    aff_ref[0] = e / jnp.sum(e, axis=0, keepdims=True)


def _merge_call(x, of, ob, gr, oatt, g1, sc2, sh2, w):
    b, l, d = x.shape
    t = min(l, 512)
    full = lambda arr: pl.BlockSpec(arr.shape, lambda bi, i: (0,) * arr.ndim)
    row = lambda n: pl.BlockSpec((1, t, n), lambda bi, i: (bi, i, 0))
    vec = pl.BlockSpec((1, 1, d), lambda bi, i: (bi, 0, 0))
    consts = [w['g_gla'], w['bsum64'], w['w_out'], w['ln1_g'], w['ln1_b'], w['wr_hi'], w['wr_lo']]
    return pl.pallas_call(
        _merge_kernel,
        grid=(b, l // t),
        in_specs=[row(d), row(256), row(256), row(256), row(768), vec, vec, vec] + [full(a) for a in consts],
        out_specs=[row(d), row(d), pl.BlockSpec((1, N_EXPERTS, t), lambda bi, i: (bi, 0, i))],
        out_shape=[jax.ShapeDtypeStruct((b, l, d), F32), jax.ShapeDtypeStruct((b, l, d), BF16),
                   jax.ShapeDtypeStruct((b, N_EXPERTS, l), F32)],
        compiler_params=_cparams(("arbitrary", "arbitrary")),
    )(x, of, ob, gr, oatt, g1, sc2, sh2, *consts)


def _topk_kernel(aff_ref, slot_ref, off_ref, *, nc, cap):
    shift = int(math.log2(nc))
    bits = pltpu.bitcast(aff_ref[0], jnp.int32)
    r = bits.shape[0]
    ri = lax.broadcasted_iota(jnp.int32, (r, r), 0)
    ci = lax.broadcasted_iota(jnp.int32, (r, r), 1)
    same = (ri >> shift) == (ci >> shift)
    ones_bd = jnp.where(same, 1.0, 0.0).astype(BF16)
    lstrict = jnp.where(jnp.logical_and(same, ci < ri), 1.0, 0.0).astype(BF16)
    li = lax.broadcasted_iota(jnp.int32, (LANE, LANE), 0)
    lj = lax.broadcasted_iota(jnp.int32, (LANE, LANE), 1)
    uex = jnp.where(li < lj, 1.0, 0.0).astype(BF16)
    ones = jnp.ones((LANE, LANE), BF16)

    def total(mask):
        colsum = _dot(ones_bd, jnp.where(mask, 1.0, 0.0).astype(BF16))
        return _dot(colsum.astype(BF16), ones)

    def excl_cumsum(mask):
        mb = jnp.where(mask, 1.0, 0.0).astype(BF16)
        off = _dot(lstrict, _dot(mb, ones).astype(BF16))
        return _dot(mb, uex) + off, off

    def body(it, v):
        cand = v | lax.shift_left(jnp.int32(1), jnp.int32(30) - it)
        return jnp.where(total(bits >= cand) >= cap, cand, v)

    v = lax.fori_loop(0, 31, body, jnp.zeros(bits.shape, jnp.int32))
    gt = bits > v
    eq = bits == v
    need = cap - total(gt)
    rank_eq, _ = excl_cumsum(eq)
    sel = jnp.logical_or(gt, jnp.logical_and(eq, rank_eq < need))
    pos, off = excl_cumsum(sel)
    slot_ref[0] = jnp.where(sel, pos.astype(jnp.int32), -1)
    off_ref[0] = off.astype(jnp.int32)


def _topk_call(aff):
    b, e, l = aff.shape
    nc = l // LANE
    cap = EC_CAPACITY_FACTOR * l // N_EXPERTS
    r = e * nc
    spec = pl.BlockSpec((1, r, LANE), lambda bi: (bi, 0, 0))
    return pl.pallas_call(
        functools.partial(_topk_kernel, nc=nc, cap=cap),
        grid=(b,),
        in_specs=[spec],
        out_specs=[spec, spec],
        out_shape=[jax.ShapeDtypeStruct((b, r, LANE), jnp.int32)] * 2,
        compiler_params=_cparams(("arbitrary",)),
    )(aff.reshape(b, r, LANE))


def _expert_kernel(s0_ref, h_ref, slot_ref, aff_ref, wg_ref, wu_ref, wd_ref, ohi_ref, olo_ref, xs_scr, gate_scr,
                   *, nb, nsub, nc, cap, win):
    e, b, j = pl.program_id(0), pl.program_id(1), pl.program_id(2)

    @pl.when(j == 0)
    def _():
        xs_scr[...] = jnp.zeros(xs_scr.shape, F32)
        gate_scr[...] = jnp.zeros(gate_scr.shape, F32)

    base = (b * N_EXPERTS + e) * nc + j * nsub
    rows = lax.broadcasted_iota(jnp.int32, (win, LANE), 0)
    for u in range(nsub):
        s0a = pl.multiple_of((s0_ref[base + u] >> 3) << 3, 8)
        lanes = slice(u * LANE, (u + 1) * LANE)
        hit = (rows + s0a) == slot_ref[0, :, lanes]
        comp = _dot(jnp.where(hit, 1.0, 0.0).astype(BF16), h_ref[0, lanes, :])
        xs_scr[pl.ds(s0a, win), :] += comp
        gate_scr[pl.ds(s0a, win), :] += jnp.sum(jnp.where(hit, aff_ref[0, :, lanes], 0.0), axis=1, keepdims=True)

    @pl.when(j == nb - 1)
    def _():
        rc = min(cap, 256)
        wg, wu, wd = (r[0, 0].astype(BF16) for r in (wg_ref, wu_ref, wd_ref))
        for c0 in range(0, cap, rc):
            xs = xs_scr[c0:c0 + rc, :].astype(BF16)
            gg = _dot(xs, wg)
            hid = (gg * (1.0 / (1.0 + jnp.exp(-gg)))) * _dot(xs, wu)
            out = _dot(hid.astype(BF16), wd) * gate_scr[c0:c0 + rc, :]
            hi, lo = _split2(out)
            ohi_ref[0, 0, c0:c0 + rc, :] = hi
            olo_ref[0, 0, c0:c0 + rc, :] = lo
        pad = jnp.zeros((ohi_ref.shape[2] - cap, ohi_ref.shape[3]), BF16)
        ohi_ref[0, 0, cap:, :] = pad
        olo_ref[0, 0, cap:, :] = pad


def _expert_call(h2, slot, aff, s0, layer, wg, wu, wd):
    b, l, d = h2.shape
    nc = l // LANE
    cap = EC_CAPACITY_FACTOR * l // N_EXPERTS
    tb = min(l, 1024)
    nsub = tb // LANE
    nb = l // tb
    win = LANE + 8
    f = wg.shape[3]
    row3 = lambda e, bi, j, s: ((bi * N_EXPERTS + e) * nb + j, 0, 0)
    grid_spec = pltpu.PrefetchScalarGridSpec(
        num_scalar_prefetch=1,
        grid=(N_EXPERTS, b, nb),
        in_specs=[pl.BlockSpec((1, tb, d), lambda e, bi, j, s: (bi, j, 0)),
                  pl.BlockSpec((1, 1, tb), row3),
                  pl.BlockSpec((1, 1, tb), row3),
                  pl.BlockSpec((1, 1, d, f), lambda e, bi, j, s: (layer, e, 0, 0)),
                  pl.BlockSpec((1, 1, d, f), lambda e, bi, j, s: (layer, e, 0, 0)),
                  pl.BlockSpec((1, 1, f, d), lambda e, bi, j, s: (layer, e, 0, 0))],
        out_specs=[pl.BlockSpec((1, 1, cap + COMBINE_WIN, d), lambda e, bi, j, s: (bi, e, 0, 0))] * 2,
        scratch_shapes=[pltpu.VMEM((cap + win, d), F32), pltpu.VMEM((cap + win, 1), F32)],
    )
    return pl.pallas_call(
        functools.partial(_expert_kernel, nb=nb, nsub=nsub, nc=nc, cap=cap, win=win),
        grid_spec=grid_spec,
        out_shape=[jax.ShapeDtypeStruct((b, N_EXPERTS, cap + COMBINE_WIN, d), BF16)] * 2,
        compiler_params=_cparams(("arbitrary", "arbitrary", "arbitrary")),
    )(s0, h2, slot.reshape(b * N_EXPERTS * nb, 1, tb), aff.reshape(b * N_EXPERTS * nb, 1, tb), wg, wu, wd)


COMBINE_WIN = LANE + 16


COMBINE_BASE = 48
COMBINE_EXT = COMBINE_WIN - COMBINE_BASE


def _combine_kernel(s0_ref, end_ref, ohi_hbm, olo_hbm, slot_ref, x1_ref, g2_ref, lng_ref, lnb_ref, o_ref,
                    hbase, lbase, hext, lext, sem, *, nc, nsteps):
    b, j = pl.program_id(0), pl.program_id(1)
    t = b * nc + j
    cur = t % 2

    def win_start(bb, jj, e):
        return pl.multiple_of((s0_ref[(bb * N_EXPERTS + e) * nc + jj] >> 4) << 4, 16)

    def needs_ext(bb, jj, e):
        return end_ref[(bb * N_EXPERTS + e) * nc + jj] > win_start(bb, jj, e) + COMBINE_BASE

    def base_copies(bb, jj, sl):
        out = []
        for e in range(N_EXPERTS):
            rows = pl.ds(win_start(bb, jj, e), COMBINE_BASE)
            dst = pl.ds(e * COMBINE_BASE, COMBINE_BASE)
            out.append(pltpu.make_async_copy(ohi_hbm.at[bb, e, rows, :], hbase.at[sl, dst, :], sem.at[sl, 0]))
            out.append(pltpu.make_async_copy(olo_hbm.at[bb, e, rows, :], lbase.at[sl, dst, :], sem.at[sl, 1]))
        return out

    def ext_copies(bb, jj, sl, e):
        rows = pl.ds(win_start(bb, jj, e) + COMBINE_BASE, COMBINE_EXT)
        dst = pl.ds(e * COMBINE_EXT, COMBINE_EXT)
        return [pltpu.make_async_copy(ohi_hbm.at[bb, e, rows, :], hext.at[sl, dst, :], sem.at[sl, 2]),
                pltpu.make_async_copy(olo_hbm.at[bb, e, rows, :], lext.at[sl, dst, :], sem.at[sl, 3])]

    def start_all(bb, jj, sl):
        for cp in base_copies(bb, jj, sl):
            cp.start()
        for e in range(N_EXPERTS):
            @pl.when(needs_ext(bb, jj, e))
            def _():
                for cp in ext_copies(bb, jj, sl, e):
                    cp.start()

    @pl.when(t == 0)
    def _():
        hext[...] = jnp.zeros(hext.shape, BF16)
        lext[...] = jnp.zeros(lext.shape, BF16)
        start_all(b, j, cur)

    @pl.when(t + 1 < nsteps)
    def _():
        wrap = j + 1 == nc
        start_all(jnp.where(wrap, b + 1, b), jnp.where(wrap, 0, j + 1), 1 - cur)

    for cp in base_copies(b, j, cur):
        cp.wait()
    any_ext = needs_ext(b, j, 0)
    for e in range(N_EXPERTS):
        any_ext = jnp.logical_or(any_ext, needs_ext(b, j, e))

        @pl.when(needs_ext(b, j, e))
        def _():
            for cp in ext_copies(b, j, cur, e):
                cp.wait()

    def onehot(nrows, offset):
        rows = lax.broadcasted_iota(jnp.int32, (nrows, LANE), 0) + offset
        return jnp.concatenate(
            [jnp.where((rows + win_start(b, j, e)) == slot_ref[0, e:e + 1, :], 1.0, 0.0).astype(BF16)
             for e in range(N_EXPERTS)], axis=0)

    p = onehot(COMBINE_BASE, 0)
    z = DEEPNORM_ALPHA * x1_ref[0] + g2_ref[0] * (_dot_tn(p, hbase[cur]) + _dot_tn(p, lbase[cur]))
    o_ref[0] = _layer_norm(z, lng_ref[...], lnb_ref[...])

    @pl.when(any_ext)
    def _():
        pe = onehot(COMBINE_EXT, COMBINE_BASE)
        z2 = z + g2_ref[0] * (_dot_tn(pe, hext[cur]) + _dot_tn(pe, lext[cur]))
        o_ref[0] = _layer_norm(z2, lng_ref[...], lnb_ref[...])


def _combine_call(x1, ohi, olo, slot, s0, end, g2, lng, lnb):
    b, l, d = x1.shape
    nc = l // LANE
    full = lambda arr: pl.BlockSpec(arr.shape, lambda bi, j, s, en: (0,) * arr.ndim)
    grid_spec = pltpu.PrefetchScalarGridSpec(
        num_scalar_prefetch=2,
        grid=(b, nc),
        in_specs=[pl.BlockSpec(memory_space=pl.ANY), pl.BlockSpec(memory_space=pl.ANY),
                  pl.BlockSpec((1, N_EXPERTS, LANE), lambda bi, j, s, en: (bi, 0, j)),
                  pl.BlockSpec((1, LANE, d), lambda bi, j, s, en: (bi, j, 0)),
                  pl.BlockSpec((1, 1, d), lambda bi, j, s, en: (bi, 0, 0)),
                  full(lng), full(lnb)],
        out_specs=pl.BlockSpec((1, LANE, d), lambda bi, j, s, en: (bi, j, 0)),
        scratch_shapes=[pltpu.VMEM((2, N_EXPERTS * COMBINE_BASE, d), BF16),
                        pltpu.VMEM((2, N_EXPERTS * COMBINE_BASE, d), BF16),
                        pltpu.VMEM((2, N_EXPERTS * COMBINE_EXT, d), BF16),
                        pltpu.VMEM((2, N_EXPERTS * COMBINE_EXT, d), BF16),
                        pltpu.SemaphoreType.DMA((2, 4))],
    )
    return pl.pallas_call(
        functools.partial(_combine_kernel, nc=nc, nsteps=b * nc),
        grid_spec=grid_spec,
        out_shape=jax.ShapeDtypeStruct((b, l, d), F32),
        compiler_params=_cparams(("arbitrary", "arbitrary")),
    )(s0, end, ohi, olo, slot.reshape(b, N_EXPERTS, l), x1, g2, lng, lnb)


def _swap_halves(w, nheads, dh):
    q = dh // 4
    return jnp.flip(w.reshape(w.shape[0], nheads, 2, 2, q), axis=3).reshape(w.shape[0], nheads * dh)


def _prep_layer(l, w_in, w_gla_gate_f, b_gla_gate_f, w_gla_gate_b, b_gla_gate_b, g_gla_norm, g_mla_qa, w_mla_qb,
                g_mla_kva, w_mla_kvb, g_gqa_q, g_gqa_k, w_out, ln1_g, ln1_b, w_router, ln2_g, ln2_b):
    wi = w_in[l]
    d = wi.shape[0]
    o = IN_OFF
    aq, ak, av = wi[:, o[9]:o[10]], wi[:, o[10]:o[11]], wi[:, o[11]:o[12]]
    mkr = wi[:, o[8]:o[9]]
    zeros = lambda n: jnp.zeros((d, n), F32)
    g_blk = jnp.concatenate([wi[:, o[4]:o[6]], zeros(32), mkr, zeros(32)], axis=1)
    gs_blk = jnp.concatenate([zeros(64), _swap_halves(mkr, 1, MLA_ROPE), zeros(32)], axis=1)
    w_in_r = jnp.concatenate([
        wi[:, 0:o[4]], wi[:, o[6]:o[7]], wi[:, o[7]:o[8]],
        aq, ak, av, g_blk,
        _swap_halves(aq, GQA_HEADS, GQA_DH), _swap_halves(ak, GQA_KV_HEADS, GQA_DH), gs_blk], axis=1).astype(BF16)
    assert w_in_r.shape[1] == C_END

    r = GLA_GATE_RANK
    wg = jnp.zeros((LANE, 256), F32)
    wg = wg.at[0:r, 0:128].set(w_gla_gate_f[l]).at[r:2 * r, 128:256].set(w_gla_gate_b[l]).astype(BF16)
    bg = jnp.concatenate([b_gla_gate_f[l], b_gla_gate_b[l]])[None, :]

    qb = w_mla_qb[l].reshape(MLA_Q_RANK, MLA_HEADS, MLA_NOPE + MLA_ROPE)
    qb_rope_sw = _swap_halves(qb[:, :, MLA_NOPE:].reshape(MLA_Q_RANK, -1), MLA_HEADS, MLA_ROPE).reshape(
        MLA_Q_RANK, MLA_HEADS, MLA_ROPE)
    qb_sw = jnp.concatenate([jnp.zeros_like(qb[:, :, :MLA_NOPE]), qb_rope_sw], axis=2)
    padq = lambda t: jnp.pad(t, ((0, 0), (0, 0), (0, HEAD_PAD - t.shape[2]))).reshape(MLA_Q_RANK, -1)
    wqb = jnp.concatenate([padq(qb), padq(qb_sw)], axis=1).astype(BF16)

    kvb = w_mla_kvb[l].reshape(MLA_KV_RANK, MLA_HEADS, MLA_NOPE + MLA_DV)
    kpart = jnp.pad(kvb[:, :, :MLA_NOPE], ((0, 0), (0, 0), (0, HEAD_PAD - MLA_NOPE))).reshape(MLA_KV_RANK, -1)
    wkvb = jnp.concatenate([kpart, kvb[:, :, MLA_NOPE:].reshape(MLA_KV_RANK, -1)], axis=1).astype(BF16)

    def gain_rows(gv, nheads):
        plain = jnp.tile(gv[None, :], (1, nheads))
        swapped = _swap_halves(plain, nheads, GQA_DH)
        width = GQA_HEADS * GQA_DH
        return [jnp.pad(t, ((0, 0), (0, width - t.shape[1]))) for t in (plain, swapped)]

    gains = jnp.concatenate(gain_rows(g_gqa_q[l], GQA_HEADS) + gain_rows(g_gqa_k[l], GQA_KV_HEADS)
                            + [jnp.zeros((4, GQA_HEADS * GQA_DH), F32)], axis=0)
    jj = np.arange(GLA_WIDTH)
    bsum64 = jnp.asarray((jj[:, None] // GLA_DV) == (jj[None, :] // GLA_DV), BF16)
    wr = w_router[l].T
    wr_hi = wr.astype(BF16)
    wr_lo = (wr - wr_hi.astype(F32)).astype(BF16)
    return dict(w_in=w_in_r, wg=wg, bg=bg, g_qa=g_mla_qa[l][None, :], wqb=wqb, g_kva=g_mla_kva[l][None, :],
                wkvb=wkvb, gains=gains, g_gla=jnp.tile(g_gla_norm[l][None, :], (1, GLA_HEADS)),
                bsum64=bsum64, w_out=w_out[l].astype(BF16), ln1_g=ln1_g[l][None, :], ln1_b=ln1_b[l][None, :],
                wr_hi=wr_hi, wr_lo=wr_lo, ln2_g=ln2_g[l][None, :], ln2_b=ln2_b[l][None, :])


def _tables(l, with_pos):
    sm = MLA_SCALE * LOG2E
    sg = GQA_SCALE * LOG2E
    z = lambda n: jnp.zeros((l, n), F32)
    one = lambda n: jnp.ones((l, n), F32)
    if with_pos:
        t = jnp.arange(l, dtype=jnp.int32)
        row = (t // GRID_W).astype(F32)[:, None]
        col = (t % GRID_W).astype(F32)[:, None]

        def cs(nfreq):
            inv = ROPE_THETA ** (-jnp.arange(nfreq, dtype=F32) / nfreq)
            ar, ac = row * inv, col * inv
            c = jnp.concatenate([jnp.cos(ar), jnp.cos(ar), jnp.cos(ac), jnp.cos(ac)], axis=1)
            s = jnp.concatenate([-jnp.sin(ar), jnp.sin(ar), -jnp.sin(ac), jnp.sin(ac)], axis=1)
            return c, s

        c32, s32 = cs(MLA_ROPE // 4)
        c64, s64 = cs(GQA_DH // 4)
    else:
        c32, s32, c64, s64 = one(32), z(32), one(64), z(64)
    cm = jnp.concatenate([one(64), c32, z(32)], axis=1) * sm
    smm = jnp.concatenate([z(64), s32, z(32)], axis=1) * sm
    ck = jnp.concatenate([z(64), c32, z(32)], axis=1)
    sk = jnp.concatenate([z(64), s32, z(32)], axis=1)
    cg = jnp.concatenate([c64, c64], axis=1)
    sgg = jnp.concatenate([s64, s64], axis=1)
    return jnp.concatenate([cm, smm, ck, sk, cg * sg, sgg * sg, cg, sgg], axis=1)


def _heads_t(a, nheads, dh):
    b, l, _ = a.shape
    return a.reshape(b, l, nheads, dh).transpose(0, 2, 3, 1)


def _heads(a, nheads, dh):
    b, l, _ = a.shape
    return a.reshape(b, l, nheads, dh).transpose(0, 2, 1, 3)


def _attend(q, qg, k, kg, v):
    b, lq, _ = q.shape
    vt = _heads_t(v, N_KVHEADS, ATT_DV)
    extra = jnp.zeros(vt.shape[:2] + (ATT_DVP - ATT_DV, vt.shape[3]), BF16).at[:, :, 0, :].set(1.0)
    vt = jnp.concatenate([vt, extra], axis=2)
    ot_mla = _attn_call(_heads_t(q, MLA_HEADS, HEAD_PAD), _heads(k, MLA_HEADS, HEAD_PAD), vt[:, :MLA_HEADS], 1)
    ot_gqa = _attn_call(_heads_t(qg, GQA_HEADS, GQA_DH), _heads(kg, GQA_KV_HEADS, GQA_DH), vt[:, MLA_HEADS:],
                        GQA_GROUP)
    ot = jnp.concatenate([ot_mla, ot_gqa], axis=1)
    return ot.transpose(0, 3, 1, 2).reshape(b, lq, N_QHEADS * ATT_DV)


def _moe(h2, aff, x1, g2, w, layer, wexp):
    b, l, _ = h2.shape
    nc = l // LANE
    cap = EC_CAPACITY_FACTOR * l // N_EXPERTS
    slot, off = _topk_call(aff)
    s0 = off[:, :, 0].reshape(-1)
    ohi, olo = _expert_call(h2, slot, aff, s0, layer, *wexp)
    starts = s0.reshape(b, N_EXPERTS, nc)
    end = jnp.concatenate([starts[:, :, 1:], jnp.full((b, N_EXPERTS, 1), cap, jnp.int32)], axis=2).reshape(-1)
    return _combine_call(x1, ohi, olo, slot, s0, end, g2, w['ln2_g'], w['ln2_b'])


def kernel(x, c, ctx, c_ctx, w_mod, b_mod, w_in, w_gla_gate_f, b_gla_gate_f, w_gla_gate_b, b_gla_gate_b, g_gla_norm, g_mla_qa, w_mla_qb, g_mla_kva, w_mla_kvb, g_gqa_q, g_gqa_k, w_out, ln1_g, ln1_b, w_router, w_exp_gate, w_exp_up, w_exp_down, ln2_g, ln2_b):
    b, l, d = x.shape
    lc = ctx.shape[1]
    assert b < 8 and l % 256 == 0 and lc % 256 == 0
    cvec = jnp.zeros((8, d), F32).at[0:b].set(c).at[b].set(c_ctx)
    mods = _mod_call(cvec, w_mod, b_mod)
    tab_l = _tables(l, True)
    tab_c = _tables(lc, False)
    zero_state = jnp.zeros((b, GLA_WIDTH, 128), F32)
    cx = ctx
    for li in range(DEPTH):
        need_ctx = li < DEPTH - 1
        w = _prep_layer(li, w_in, w_gla_gate_f, b_gla_gate_f, w_gla_gate_b, b_gla_gate_b, g_gla_norm, g_mla_qa,
                        w_mla_qb, g_mla_kva, w_mla_kvb, g_gqa_q, g_gqa_k, w_out, ln1_g, ln1_b, w_router, ln2_g,
                        ln2_b)
        wexp = (w_exp_gate, w_exp_up, w_exp_down)
        sh1, sc1, g1, sh2, sc2, g2 = [m[:, None, :] for m in jnp.split(mods[li, 0:b], 6, axis=-1)]
        csh1, csc1, cg1, csh2, csc2, cg2 = [jnp.broadcast_to(m[:, None, :], (b, 1, d))
                                            for m in jnp.split(mods[li, b:b + 1], 6, axis=-1)]
        names = ('gq', 'gk', 'la', 'gv', 'gr', 'q', 'qg', 'k', 'kg', 'v')
        pl_ = dict(zip(names, _proj_call(x, sc1, sh1, tab_l, w)))
        pc_ = dict(zip(names, _proj_call(cx, csc1, csh1, tab_c, w)))
        ocf, s_f = _gla_call(pc_, zero_state, 0)
        ocb, s_b = _gla_call(pc_, zero_state, 1)
        olf, _ = _gla_call(pl_, s_f, 0)
        olb, _ = _gla_call(pl_, s_b, 1)
        both = lambda name: jnp.concatenate([pc_[name], pl_[name]], axis=1)
        oatt = _attend(pl_['q'], pl_['qg'], both('k'), both('kg'), both('v'))
        x1, h2, aff = _merge_call(x, olf, olb, pl_['gr'], oatt, g1, sc2, sh2, w)
        x = _moe(h2, aff, x1, g2, w, li, wexp)
        if need_ctx:
            oatt_c = _attend(pc_['q'], pc_['qg'], pc_['k'], pc_['kg'], pc_['v'])
            c1, ch2, caff = _merge_call(cx, ocf, ocb, pc_['gr'], oatt_c, cg1, csc2, csh2, w)
            cx = _moe(ch2, caff, c1, cg2, w, li, wexp)
    return x
```

```python
import functools
import math

import numpy as np
import jax
import jax.numpy as jnp
from jax import lax
from jax.experimental import pallas as pl
from jax.experimental.pallas import tpu as pltpu

F32 = jnp.float32
BF16 = jnp.bfloat16

D_MODEL = 1024
DEPTH = 4
GRID_W = 64
ROPE_THETA = 10000.0
LN_EPS = 1e-6
RMS_EPS = 1e-6
DEEPNORM_ALPHA = (2 * DEPTH) ** 0.25

GLA_HEADS, GLA_DK, GLA_DV, GLA_GATE_RANK, GLA_TAU, GLA_CHUNK = 4, 32, 64, 16, 16.0, 64
MLA_HEADS, MLA_Q_RANK, MLA_KV_RANK, MLA_NOPE, MLA_ROPE, MLA_DV = 6, 384, 256, 64, 32, 64
MLA_SCALE = (MLA_NOPE + MLA_ROPE) ** -0.5
GQA_HEADS, GQA_KV_HEADS, GQA_DH = 6, 2, 64
GQA_GROUP = GQA_HEADS // GQA_KV_HEADS
GQA_SCALE = GQA_DH ** -0.5
N_EXPERTS = 16
EC_CAPACITY_FACTOR = 2
D_EXPERT = 1024

GLA_WIDTH = GLA_HEADS * GLA_DV
IN_SPLITS = (GLA_HEADS * GLA_DK, GLA_HEADS * GLA_DK, GLA_WIDTH, GLA_WIDTH, GLA_GATE_RANK, GLA_GATE_RANK,
             MLA_Q_RANK, MLA_KV_RANK, MLA_ROPE,
             GQA_HEADS * GQA_DH, GQA_KV_HEADS * GQA_DH, GQA_KV_HEADS * GQA_DH)
IN_OFF = [0] + [int(o) for o in np.cumsum(IN_SPLITS)]

LANE = 128
HEAD_PAD = 128
N_QHEADS = MLA_HEADS + GQA_HEADS
N_KVHEADS = MLA_HEADS + GQA_KV_HEADS
ATT_DV = 64
ATT_DVP = 80
ATT_SUB = 128
ATT_AHEAD = 2
LOG2E = math.log2(math.e)
VMEM_LIMIT = 56 * 1024 * 1024

C_A, C_MQA, C_MKVA, C_AQ, C_AK, C_AV, C_G, C_AQS, C_AKS, C_GS, C_END = (
    0, 768, 1152, 1408, 1792, 1920, 2048, 2176, 2560, 2688, 2816)


def _cparams(sem):
    return pltpu.CompilerParams(dimension_semantics=sem, vmem_limit_bytes=VMEM_LIMIT)


def _dot(a, b):
    return jnp.dot(a, b, preferred_element_type=F32)


def _dot_nt(a, b):
    return lax.dot_general(a, b, (((1,), (1,)), ((), ())), preferred_element_type=F32)


def _dot_tn(a, b):
    return lax.dot_general(a, b, (((0,), (0,)), ((), ())), preferred_element_type=F32)


def _split2(x):
    hi = x.astype(BF16)
    lo = (x - hi.astype(F32)).astype(BF16)
    return hi, lo


def _split3(x):
    hi = x.astype(BF16)
    r = x - hi.astype(F32)
    mid = r.astype(BF16)
    lo = (r - mid.astype(F32)).astype(BF16)
    return hi, mid, lo


def _layer_norm(z, g, b):
    mu = jnp.mean(z, axis=-1, keepdims=True)
    zc = z - mu
    var = jnp.mean(zc * zc, axis=-1, keepdims=True)
    return zc * lax.rsqrt(var + LN_EPS) * g + b


def _mod_kernel(c_ref, w_ref, b_ref, o_ref):
    cv = c_ref[...]
    s = cv * (1.0 / (1.0 + jnp.exp(-cv)))
    o_ref[0] = _dot(s.astype(BF16), w_ref[0].astype(BF16)) + b_ref[0]


def _mod_call(cvec, w_mod, b_mod):
    depth, d, n6 = w_mod.shape
    bn = 768
    return pl.pallas_call(
        _mod_kernel,
        grid=(depth, n6 // bn),
        in_specs=[pl.BlockSpec((8, d), lambda l, j: (0, 0)),
                  pl.BlockSpec((1, d, bn), lambda l, j: (l, 0, j)),
                  pl.BlockSpec((1, 1, bn), lambda l, j: (l, 0, j))],
        out_specs=pl.BlockSpec((1, 8, bn), lambda l, j: (l, 0, j)),
        out_shape=jax.ShapeDtypeStruct((depth, 8, n6), F32),
        compiler_params=_cparams(("arbitrary", "arbitrary")),
    )(cvec, w_mod, b_mod.reshape(depth, 1, n6))


def _proj_kernel(x_ref, sc_ref, sh_ref, tab_ref, w_ref, wg_ref, bg_ref, gqa_ref, wqb_ref, gkva_ref, wkvb_ref,
                 gains_ref,
                 gq_ref, gk_ref, la_ref, gv_ref, gr_ref, q_ref, qg_ref, k_ref, kg_ref, v_ref):
    h = x_ref[0] * (1.0 + sc_ref[0]) + sh_ref[0]
    hb = h.astype(BF16)

    def mm(a, b):
        return _dot(hb, w_ref[:, a:b])

    def tab(i):
        return tab_ref[:, i * LANE:(i + 1) * LANE]

    a = mm(C_A, C_MQA)
    gq_ref[0] = a[:, 0:128] * (GLA_DK ** -0.5)
    gk_ref[0] = a[:, 128:256]
    gv_ref[0] = a[:, 256:512]
    gr_ref[0] = a[:, 512:768]

    g = mm(C_G, C_AQS)
    gs = mm(C_GS, C_END)
    lg = _dot(g.astype(BF16), wg_ref[...]) + bg_ref[...]
    la_ref[0] = (jnp.minimum(lg, 0.0) - jnp.log1p(jnp.exp(-jnp.abs(lg)))) * (1.0 / GLA_TAU)

    mqa = mm(C_MQA, C_MKVA)
    nq = mqa * lax.rsqrt(jnp.mean(mqa * mqa, axis=-1, keepdims=True) + RMS_EPS) * gqa_ref[...]
    q2 = _dot(nq.astype(BF16), wqb_ref[...])
    cm, sm = tab(0), tab(1)
    for hh in range(MLA_HEADS):
        lo = hh * HEAD_PAD
        qh = q2[:, lo:lo + HEAD_PAD] * cm + q2[:, 768 + lo:768 + lo + HEAD_PAD] * sm
        q_ref[0, :, lo:lo + HEAD_PAD] = qh.astype(BF16)

    mkva = mm(C_MKVA, C_AQ)
    nkv = mkva * lax.rsqrt(jnp.mean(mkva * mkva, axis=-1, keepdims=True) + RMS_EPS) * gkva_ref[...]
    kv = _dot(nkv.astype(BF16), wkvb_ref[...])
    kr = g * tab(2) + gs * tab(3)
    for hh in range(MLA_HEADS):
        lo = hh * HEAD_PAD
        k_ref[0, :, lo:lo + HEAD_PAD] = (kv[:, lo:lo + HEAD_PAD] + kr).astype(BF16)
    v_ref[0, :, 0:384] = kv[:, 768:1152].astype(BF16)

    first = lax.broadcasted_iota(jnp.int32, (1, LANE), 1) < GQA_DH

    def pair_rs(t):
        sq = t * t
        s_lo = jnp.sum(jnp.where(first, sq, 0.0), axis=-1, keepdims=True)
        s_hi = jnp.sum(jnp.where(first, 0.0, sq), axis=-1, keepdims=True)
        return lax.rsqrt(jnp.where(first, s_lo, s_hi) * (1.0 / GQA_DH) + RMS_EPS)

    def normed_rotary(x, xs, g_row, gs_row, cos, sin):
        rs = pair_rs(x)
        return x * rs * g_row * cos + xs * rs * gs_row * sin

    aq = mm(C_AQ, C_AK)
    aqs = mm(C_AQS, C_AKS)
    for pp in range(GQA_HEADS // 2):
        sl = slice(pp * LANE, (pp + 1) * LANE)
        qh = normed_rotary(aq[:, sl], aqs[:, sl], gains_ref[0:1, sl], gains_ref[1:2, sl], tab(4), tab(5))
        qg_ref[0, :, sl] = qh.astype(BF16)
    kh = normed_rotary(mm(C_AK, C_AV), mm(C_AKS, C_GS), gains_ref[2:3, 0:LANE], gains_ref[3:4, 0:LANE],
                       tab(6), tab(7))
    kg_ref[0] = kh.astype(BF16)
    v_ref[0, :, 384:512] = mm(C_AV, C_G).astype(BF16)


def _proj_call(x, sc, sh, tab, w):
    b, l, d = x.shape
    t = min(l, 512)
    full = lambda arr: pl.BlockSpec(arr.shape, lambda bi, i: (0,) * arr.ndim)
    row = lambda n: pl.BlockSpec((1, t, n), lambda bi, i: (bi, i, 0))
    vec = pl.BlockSpec((1, 1, d), lambda bi, i: (bi, 0, 0))
    consts = [w['w_in'], w['wg'], w['bg'], w['g_qa'], w['wqb'], w['g_kva'], w['wkvb'], w['gains']]
    outs = [(128, F32), (128, F32), (256, F32), (256, F32), (256, F32),
            (768, BF16), (384, BF16), (768, BF16), (128, BF16), (512, BF16)]
    return pl.pallas_call(
        _proj_kernel,
        grid=(b, l // t),
        in_specs=[row(d), vec, vec, pl.BlockSpec((t, 8 * LANE), lambda bi, i: (i, 0))] + [full(a) for a in consts],
        out_specs=[row(n) for n, _ in outs],
        out_shape=[jax.ShapeDtypeStruct((b, l, n), dt) for n, dt in outs],
        compiler_params=_cparams(("arbitrary", "arbitrary")),
    )(x, sc, sh, tab, *consts)


GLA_BLOCK = 256


def _gla_kernel(q_ref, k_ref, la_ref, v_ref, s0_ref, o_ref, sfin_ref, s_scr, *, reverse, nblk):
    i = pl.program_id(1)

    @pl.when(i == 0)
    def _():
        s_scr[...] = s0_ref[0]

    r = q_ref.shape[1]
    nch = r // GLA_CHUNK
    la = la_ref[0]
    q = q_ref[0]
    k = k_ref[0]
    v = v_ref[0]
    ri = lax.broadcasted_iota(jnp.int32, (r, r), 0)
    ci = lax.broadcasted_iota(jnp.int32, (r, r), 1)
    same = (ri >> 6) == (ci >> 6)
    tri = jnp.logical_and(same, (ci >= ri) if reverse else (ci <= ri))
    trib = jnp.where(tri, 1.0, 0.0).astype(BF16)
    oneb = jnp.where(same, 1.0, 0.0).astype(BF16)
    hi, mid, lo = _split3(la)
    bcum = _dot(trib, hi) + _dot(trib, mid) + _dot(trib, lo)
    tot = _dot(oneb, hi) + _dot(oneb, mid) + _dot(oneb, lo)
    qd = q * jnp.exp(bcum)
    kinv = (k * jnp.exp(-bcum)).astype(BF16)
    kend = (k * jnp.exp(tot - bcum)).astype(BF16)
    dec = jnp.exp(tot)
    vb = v.astype(BF16)
    lane_k = lax.broadcasted_iota(jnp.int32, (1, GLA_HEADS * GLA_DK), 1)
    lane_v = lax.broadcasted_iota(jnp.int32, (1, GLA_WIDTH), 1)
    o = jnp.zeros((r, GLA_WIDTH), F32)
    for hh in range(GLA_HEADS):
        qh = jnp.where((lane_k >> 5) == hh, qd, 0.0).astype(BF16)
        att = jnp.where(tri, _dot_nt(qh, kinv), 0.0)
        vh = jnp.where((lane_v >> 6) == hh, v, 0.0).astype(BF16)
        o = o + _dot(att.astype(BF16), vh)
    qdb = qd.astype(BF16)
    bd = (lax.broadcasted_iota(jnp.int32, (GLA_WIDTH, GLA_HEADS * GLA_DK), 0) >> 6) == (
        lax.broadcasted_iota(jnp.int32, (GLA_WIDTH, GLA_HEADS * GLA_DK), 1) >> 5)
    s = s_scr[...]
    order = range(nch - 1, -1, -1) if reverse else range(nch)
    for cc in order:
        rows = slice(cc * GLA_CHUNK, (cc + 1) * GLA_CHUNK)
        o_ref[0, rows, :] = o[rows] + _dot_nt(qdb[rows], s.astype(BF16))
        ut = _dot_tn(vb[rows], kend[rows])
        s = dec[cc * GLA_CHUNK:cc * GLA_CHUNK + 1, :] * s + jnp.where(bd, ut, 0.0)
    s_scr[...] = s

    @pl.when(i == nblk - 1)
    def _():
        sfin_ref[0] = s


def _gla_call(p, s0, direction):
    b, l, _ = p['gq'].shape
    r = min(l, GLA_BLOCK)
    nblk = l // r
    reverse = direction == 1
    blk = (lambda i: nblk - 1 - i) if reverse else (lambda i: i)
    row = lambda n, col=0: pl.BlockSpec((1, r, n), lambda bi, i: (bi, blk(i), col))
    st = pl.BlockSpec((1, GLA_WIDTH, 128), lambda bi, i: (bi, 0, 0))
    return pl.pallas_call(
        functools.partial(_gla_kernel, reverse=reverse, nblk=nblk),
        grid=(b, nblk),
        in_specs=[row(128), row(128), row(128, direction), row(256), st],
        out_specs=[row(256), st],
        out_shape=[jax.ShapeDtypeStruct((b, l, GLA_WIDTH), F32), jax.ShapeDtypeStruct((b, GLA_WIDTH, 128), F32)],
        scratch_shapes=[pltpu.VMEM((GLA_WIDTH, 128), F32)],
        compiler_params=_cparams(("arbitrary", "arbitrary")),
    )(p['gq'], p['gk'], p['la'], p['gv'], s0)


def _attn_kernel(qt_ref, k_ref, vt_ref, o_ref, m_scr, acc_scr, *, nk):
    ki = pl.program_id(3)

    @pl.when(ki == 0)
    def _():
        m_scr[...] = jnp.full(m_scr.shape, -jnp.inf, F32)
        acc_scr[...] = jnp.zeros(acc_scr.shape, F32)

    qt = qt_ref[0, 0]
    m, acc = m_scr[...], acc_scr[...]
    nsub = k_ref.shape[2] // ATT_SUB

    def scores(c):
        return _dot(k_ref[0, 0, c * ATT_SUB:(c + 1) * ATT_SUB, :], qt)

    def weighted_values(acc_in, alpha_c, p_c, c):
        return alpha_c * acc_in + _dot(vt_ref[0, 0, :, c * ATT_SUB:(c + 1) * ATT_SUB], p_c)

    ahead = [scores(c) for c in range(min(ATT_AHEAD, nsub))]
    pending = None
    for c in range(nsub):
        s = ahead.pop(0)
        if c + ATT_AHEAD < nsub:
            ahead.append(scores(c + ATT_AHEAD))
        m_new = jnp.maximum(m, jnp.max(s, axis=0, keepdims=True))
        alpha = jnp.exp2(m - m_new)
        p = jnp.exp2(s - m_new).astype(BF16)
        m = m_new
        if pending is not None:
            acc = weighted_values(acc, *pending)
        pending = (alpha, p, c)
    acc = weighted_values(acc, *pending)
    m_scr[...], acc_scr[...] = m, acc

    @pl.when(ki == nk - 1)
    def _():
        fin = acc_scr[...]
        o_ref[0, 0] = (fin[0:ATT_DV, :] / fin[ATT_DV:ATT_DV + 1, :]).astype(o_ref.dtype)


def _pick_tile(n, options):
    for t in options:
        if n % t == 0:
            return t
    raise ValueError(f"no tile for {n}")


def _attn_call(qt, k, vt, group):
    b, nh, dk, lq = qt.shape
    lk = k.shape[2]
    tq = _pick_tile(lq, (1024, 512, 256))
    tk = _pick_tile(lk, (8448, 2816, 768, 512, 256))
    nk = lk // tk

    def kvh(h):
        return h // group

    return pl.pallas_call(
        functools.partial(_attn_kernel, nk=nk),
        grid=(b, nh, lq // tq, nk),
        in_specs=[pl.BlockSpec((1, 1, dk, tq), lambda bi, h, qi, ki: (bi, h, 0, qi)),
                  pl.BlockSpec((1, 1, tk, dk), lambda bi, h, qi, ki: (bi, kvh(h), ki, 0)),
                  pl.BlockSpec((1, 1, ATT_DVP, tk), lambda bi, h, qi, ki: (bi, kvh(h), 0, ki))],
        out_specs=pl.BlockSpec((1, 1, ATT_DV, tq), lambda bi, h, qi, ki: (bi, h, 0, qi)),
        out_shape=jax.ShapeDtypeStruct((b, nh, ATT_DV, lq), BF16),
        scratch_shapes=[pltpu.VMEM((1, tq), F32), pltpu.VMEM((ATT_DVP, tq), F32)],
        compiler_params=_cparams(("arbitrary", "arbitrary", "arbitrary", "arbitrary")),
    )(qt, k, vt)


def _merge_kernel(x_ref, of_ref, ob_ref, gr_ref, oa_ref, g1_ref, sc_ref, sh_ref, ggla_ref, bs_ref, wout_ref,
                  lng_ref, lnb_ref, wrh_ref, wrl_ref, x1_ref, h2_ref, aff_ref):
    o = of_ref[0] + ob_ref[0]
    hi, lo = _split2(o * o)
    ms = (_dot(hi, bs_ref[...]) + _dot(lo, bs_ref[...])) * (1.0 / GLA_DV)
    r = gr_ref[0]
    gla = o * lax.rsqrt(ms + RMS_EPS) * ggla_ref[...] * (r * (1.0 / (1.0 + jnp.exp(-r))))
    a = _dot(gla.astype(BF16), wout_ref[0:GLA_WIDTH, :]) + _dot(oa_ref[0], wout_ref[GLA_WIDTH:, :])
    x1 = _layer_norm(DEEPNORM_ALPHA * x_ref[0] + g1_ref[0] * a, lng_ref[...], lnb_ref[...])
    x1_ref[0] = x1
    h2 = x1 * (1.0 + sc_ref[0]) + sh_ref[0]
    h2h, h2l = _split2(h2)
    h2_ref[0] = h2h
    wh, wl = wrh_ref[...], wrl_ref[...]
    logits = _dot_nt(wh, h2h) + _dot_nt(wh, h2l) + _dot_nt(wl, h2h)
    e = jnp.exp(logits - jnp.max(logits, axis=0, keepdims=True))
    aff_ref[0] = e / jnp.sum(e, axis=0, keepdims=True)


def _merge_call(x, of, ob, gr, oatt, g1, sc2, sh2, w):
    b, l, d = x.shape
    t = min(l, 512)
    full = lambda arr: pl.BlockSpec(arr.shape, lambda bi, i: (0,) * arr.ndim)
    row = lambda n: pl.BlockSpec((1, t, n), lambda bi, i: (bi, i, 0))
    vec = pl.BlockSpec((1, 1, d), lambda bi, i: (bi, 0, 0))
    consts = [w['g_gla'], w['bsum64'], w['w_out'], w['ln1_g'], w['ln1_b'], w['wr_hi'], w['wr_lo']]
    return pl.pallas_call(
        _merge_kernel,
        grid=(b, l // t),
        in_specs=[row(d), row(256), row(256), row(256), row(768), vec, vec, vec] + [full(a) for a in consts],
        out_specs=[row(d), row(d), pl.BlockSpec((1, N_EXPERTS, t), lambda bi, i: (bi, 0, i))],
        out_shape=[jax.ShapeDtypeStruct((b, l, d), F32), jax.ShapeDtypeStruct((b, l, d), BF16),
                   jax.ShapeDtypeStruct((b, N_EXPERTS, l), F32)],
        compiler_params=_cparams(("arbitrary", "arbitrary")),
    )(x, of, ob, gr, oatt, g1, sc2, sh2, *consts)


def _topk_kernel(aff_ref, slot_ref, off_ref, *, nc, cap):
    shift = int(math.log2(nc))
    bits = pltpu.bitcast(aff_ref[0], jnp.int32)
    r = bits.shape[0]
    ri = lax.broadcasted_iota(jnp.int32, (r, r), 0)
    ci = lax.broadcasted_iota(jnp.int32, (r, r), 1)
    same = (ri >> shift) == (ci >> shift)
    ones_bd = jnp.where(same, 1.0, 0.0).astype(BF16)
    lstrict = jnp.where(jnp.logical_and(same, ci < ri), 1.0, 0.0).astype(BF16)
    li = lax.broadcasted_iota(jnp.int32, (LANE, LANE), 0)
    lj = lax.broadcasted_iota(jnp.int32, (LANE, LANE), 1)
    uex = jnp.where(li < lj, 1.0, 0.0).astype(BF16)
    ones = jnp.ones((LANE, LANE), BF16)

    def total(mask):
        colsum = _dot(ones_bd, jnp.where(mask, 1.0, 0.0).astype(BF16))
        return _dot(colsum.astype(BF16), ones)

    def excl_cumsum(mask):
        mb = jnp.where(mask, 1.0, 0.0).astype(BF16)
        off = _dot(lstrict, _dot(mb, ones).astype(BF16))
        return _dot(mb, uex) + off, off

    def body(it, v):
        cand = v | lax.shift_left(jnp.int32(1), jnp.int32(30) - it)
        return jnp.where(total(bits >= cand) >= cap, cand, v)

    v = lax.fori_loop(0, 31, body, jnp.zeros(bits.shape, jnp.int32))
    gt = bits > v
    eq = bits == v
    need = cap - total(gt)
    rank_eq, _ = excl_cumsum(eq)
    sel = jnp.logical_or(gt, jnp.logical_and(eq, rank_eq < need))
    pos, off = excl_cumsum(sel)
    slot_ref[0] = jnp.where(sel, pos.astype(jnp.int32), -1)
    off_ref[0] = off.astype(jnp.int32)


def _topk_call(aff):
    b, e, l = aff.shape
    nc = l // LANE
    cap = EC_CAPACITY_FACTOR * l // N_EXPERTS
    r = e * nc
    spec = pl.BlockSpec((1, r, LANE), lambda bi: (bi, 0, 0))
    return pl.pallas_call(
        functools.partial(_topk_kernel, nc=nc, cap=cap),
        grid=(b,),
        in_specs=[spec],
        out_specs=[spec, spec],
        out_shape=[jax.ShapeDtypeStruct((b, r, LANE), jnp.int32)] * 2,
        compiler_params=_cparams(("arbitrary",)),
    )(aff.reshape(b, r, LANE))


def _expert_kernel(s0_ref, h_ref, slot_ref, aff_ref, wg_ref, wu_ref, wd_ref, o_ref, xs_scr, gate_scr,
                   *, nb, nsub, nc, cap, win):
    e, b, j = pl.program_id(0), pl.program_id(1), pl.program_id(2)

    @pl.when(j == 0)
    def _():
        xs_scr[...] = jnp.zeros(xs_scr.shape, F32)
        gate_scr[...] = jnp.zeros(gate_scr.shape, F32)

    base = (b * N_EXPERTS + e) * nc + j * nsub
    rows = lax.broadcasted_iota(jnp.int32, (win, LANE), 0)
    for u in range(nsub):
        s0a = pl.multiple_of((s0_ref[base + u] >> 3) << 3, 8)
        lanes = slice(u * LANE, (u + 1) * LANE)
        hit = (rows + s0a) == slot_ref[0, :, lanes]
        comp = _dot(jnp.where(hit, 1.0, 0.0).astype(BF16), h_ref[0, lanes, :])
        xs_scr[pl.ds(s0a, win), :] += comp
        gate_scr[pl.ds(s0a, win), :] += jnp.sum(jnp.where(hit, aff_ref[0, :, lanes], 0.0), axis=1, keepdims=True)

    @pl.when(j == nb - 1)
    def _():
        rc = min(cap, 256)
        wg, wu, wd = (r[0, 0].astype(BF16) for r in (wg_ref, wu_ref, wd_ref))
        for c0 in range(0, cap, rc):
            xs = xs_scr[c0:c0 + rc, :].astype(BF16)
            gg = _dot(xs, wg)
            hid = (gg * (1.0 / (1.0 + jnp.exp(-gg)))) * _dot(xs, wu)
            out = _dot(hid.astype(BF16), wd) * gate_scr[c0:c0 + rc, :]
            o_ref[0, 0, c0:c0 + rc, :] = out.astype(BF16)
        o_ref[0, 0, cap:, :] = jnp.zeros((o_ref.shape[2] - cap, o_ref.shape[3]), BF16)


def _expert_call(h2, slot, aff, s0, layer, wg, wu, wd):
    b, l, d = h2.shape
    nc = l // LANE
    cap = EC_CAPACITY_FACTOR * l // N_EXPERTS
    tb = min(l, 2048)
    nsub = tb // LANE
    nb = l // tb
    win = LANE + 8
    f = wg.shape[3]
    row3 = lambda e, bi, j, s: ((bi * N_EXPERTS + e) * nb + j, 0, 0)
    grid_spec = pltpu.PrefetchScalarGridSpec(
        num_scalar_prefetch=1,
        grid=(N_EXPERTS, b, nb),
        in_specs=[pl.BlockSpec((1, tb, d), lambda e, bi, j, s: (bi, j, 0)),
                  pl.BlockSpec((1, 1, tb), row3),
                  pl.BlockSpec((1, 1, tb), row3),
                  pl.BlockSpec((1, 1, d, f), lambda e, bi, j, s: (layer, e, 0, 0)),
                  pl.BlockSpec((1, 1, d, f), lambda e, bi, j, s: (layer, e, 0, 0)),
                  pl.BlockSpec((1, 1, f, d), lambda e, bi, j, s: (layer, e, 0, 0))],
        out_specs=pl.BlockSpec((1, 1, cap + COMBINE_WIN, d), lambda e, bi, j, s: (bi, e, 0, 0)),
        scratch_shapes=[pltpu.VMEM((cap + win, d), F32), pltpu.VMEM((cap + win, 1), F32)],
    )
    return pl.pallas_call(
        functools.partial(_expert_kernel, nb=nb, nsub=nsub, nc=nc, cap=cap, win=win),
        grid_spec=grid_spec,
        out_shape=jax.ShapeDtypeStruct((b, N_EXPERTS, cap + COMBINE_WIN, d), BF16),
        compiler_params=_cparams(("arbitrary", "arbitrary", "arbitrary")),
    )(s0, h2, slot.reshape(b * N_EXPERTS * nb, 1, tb), aff.reshape(b * N_EXPERTS * nb, 1, tb), wg, wu, wd)


COMBINE_WIN = LANE + 16


COMBINE_BASE = 48
COMBINE_EXT = COMBINE_WIN - COMBINE_BASE


def _combine_kernel(s0_ref, end_ref, eo_hbm, slot_ref, x1_ref, g2_ref, lng_ref, lnb_ref, o_ref,
                    base, ext, sem, *, nc, nsteps):
    b, j = pl.program_id(0), pl.program_id(1)
    t = b * nc + j
    cur = t % 2

    def win_start(bb, jj, e):
        return pl.multiple_of((s0_ref[(bb * N_EXPERTS + e) * nc + jj] >> 4) << 4, 16)

    def needs_ext(bb, jj, e):
        return end_ref[(bb * N_EXPERTS + e) * nc + jj] > win_start(bb, jj, e) + COMBINE_BASE

    def base_copies(bb, jj, sl):
        out = []
        for e in range(N_EXPERTS):
            rows = pl.ds(win_start(bb, jj, e), COMBINE_BASE)
            dst = pl.ds(e * COMBINE_BASE, COMBINE_BASE)
            out.append(pltpu.make_async_copy(eo_hbm.at[bb, e, rows, :], base.at[sl, dst, :], sem.at[sl, 0]))
        return out

    def ext_copies(bb, jj, sl, e):
        rows = pl.ds(win_start(bb, jj, e) + COMBINE_BASE, COMBINE_EXT)
        dst = pl.ds(e * COMBINE_EXT, COMBINE_EXT)
        return [pltpu.make_async_copy(eo_hbm.at[bb, e, rows, :], ext.at[sl, dst, :], sem.at[sl, 1])]

    def start_all(bb, jj, sl):
        for cp in base_copies(bb, jj, sl):
            cp.start()
        for e in range(N_EXPERTS):
            @pl.when(needs_ext(bb, jj, e))
            def _():
                for cp in ext_copies(bb, jj, sl, e):
                    cp.start()

    @pl.when(t == 0)
    def _():
        ext[...] = jnp.zeros(ext.shape, BF16)
        start_all(b, j, cur)

    @pl.when(t + 1 < nsteps)
    def _():
        wrap = j + 1 == nc
        start_all(jnp.where(wrap, b + 1, b), jnp.where(wrap, 0, j + 1), 1 - cur)

    for cp in base_copies(b, j, cur):
        cp.wait()
    any_ext = needs_ext(b, j, 0)
    for e in range(N_EXPERTS):
        any_ext = jnp.logical_or(any_ext, needs_ext(b, j, e))

        @pl.when(needs_ext(b, j, e))
        def _():
            for cp in ext_copies(b, j, cur, e):
                cp.wait()

    def onehot(nrows, offset):
        rows = lax.broadcasted_iota(jnp.int32, (nrows, LANE), 0) + offset
        return jnp.concatenate(
            [jnp.where((rows + win_start(b, j, e)) == slot_ref[0, e:e + 1, :], 1.0, 0.0).astype(BF16)
             for e in range(N_EXPERTS)], axis=0)

    p = onehot(COMBINE_BASE, 0)
    z = DEEPNORM_ALPHA * x1_ref[0] + g2_ref[0] * _dot_tn(p, base[cur])
    o_ref[0] = _layer_norm(z, lng_ref[...], lnb_ref[...])

    @pl.when(any_ext)
    def _():
        pe = onehot(COMBINE_EXT, COMBINE_BASE)
        z2 = z + g2_ref[0] * _dot_tn(pe, ext[cur])
        o_ref[0] = _layer_norm(z2, lng_ref[...], lnb_ref[...])


def _combine_call(x1, eo, slot, s0, end, g2, lng, lnb):
    b, l, d = x1.shape
    nc = l // LANE
    full = lambda arr: pl.BlockSpec(arr.shape, lambda bi, j, s, en: (0,) * arr.ndim)
    grid_spec = pltpu.PrefetchScalarGridSpec(
        num_scalar_prefetch=2,
        grid=(b, nc),
        in_specs=[pl.BlockSpec(memory_space=pl.ANY),
                  pl.BlockSpec((1, N_EXPERTS, LANE), lambda bi, j, s, en: (bi, 0, j)),
                  pl.BlockSpec((1, LANE, d), lambda bi, j, s, en: (bi, j, 0)),
                  pl.BlockSpec((1, 1, d), lambda bi, j, s, en: (bi, 0, 0)),
                  full(lng), full(lnb)],
        out_specs=pl.BlockSpec((1, LANE, d), lambda bi, j, s, en: (bi, j, 0)),
        scratch_shapes=[pltpu.VMEM((2, N_EXPERTS * COMBINE_BASE, d), BF16),
                        pltpu.VMEM((2, N_EXPERTS * COMBINE_EXT, d), BF16),
                        pltpu.SemaphoreType.DMA((2, 2))],
    )
    return pl.pallas_call(
        functools.partial(_combine_kernel, nc=nc, nsteps=b * nc),
        grid_spec=grid_spec,
        out_shape=jax.ShapeDtypeStruct((b, l, d), F32),
        compiler_params=_cparams(("arbitrary", "arbitrary")),
    )(s0, end, eo, slot.reshape(b, N_EXPERTS, l), x1, g2, lng, lnb)


def _swap_halves(w, nheads, dh):
    q = dh // 4
    return jnp.flip(w.reshape(w.shape[0], nheads, 2, 2, q), axis=3).reshape(w.shape[0], nheads * dh)


def _prep_layer(l, w_in, w_gla_gate_f, b_gla_gate_f, w_gla_gate_b, b_gla_gate_b, g_gla_norm, g_mla_qa, w_mla_qb,
                g_mla_kva, w_mla_kvb, g_gqa_q, g_gqa_k, w_out, ln1_g, ln1_b, w_router, ln2_g, ln2_b):
    wi = w_in[l]
    d = wi.shape[0]
    o = IN_OFF
    aq, ak, av = wi[:, o[9]:o[10]], wi[:, o[10]:o[11]], wi[:, o[11]:o[12]]
    mkr = wi[:, o[8]:o[9]]
    zeros = lambda n: jnp.zeros((d, n), F32)
    g_blk = jnp.concatenate([wi[:, o[4]:o[6]], zeros(32), mkr, zeros(32)], axis=1)
    gs_blk = jnp.concatenate([zeros(64), _swap_halves(mkr, 1, MLA_ROPE), zeros(32)], axis=1)
    w_in_r = jnp.concatenate([
        wi[:, 0:o[4]], wi[:, o[6]:o[7]], wi[:, o[7]:o[8]],
        aq, ak, av, g_blk,
        _swap_halves(aq, GQA_HEADS, GQA_DH), _swap_halves(ak, GQA_KV_HEADS, GQA_DH), gs_blk], axis=1).astype(BF16)
    assert w_in_r.shape[1] == C_END

    r = GLA_GATE_RANK
    wg = jnp.zeros((LANE, 256), F32)
    wg = wg.at[0:r, 0:128].set(w_gla_gate_f[l]).at[r:2 * r, 128:256].set(w_gla_gate_b[l]).astype(BF16)
    bg = jnp.concatenate([b_gla_gate_f[l], b_gla_gate_b[l]])[None, :]

    qb = w_mla_qb[l].reshape(MLA_Q_RANK, MLA_HEADS, MLA_NOPE + MLA_ROPE)
    qb_rope_sw = _swap_halves(qb[:, :, MLA_NOPE:].reshape(MLA_Q_RANK, -1), MLA_HEADS, MLA_ROPE).reshape(
        MLA_Q_RANK, MLA_HEADS, MLA_ROPE)
    qb_sw = jnp.concatenate([jnp.zeros_like(qb[:, :, :MLA_NOPE]), qb_rope_sw], axis=2)
    padq = lambda t: jnp.pad(t, ((0, 0), (0, 0), (0, HEAD_PAD - t.shape[2]))).reshape(MLA_Q_RANK, -1)
    wqb = jnp.concatenate([padq(qb), padq(qb_sw)], axis=1).astype(BF16)

    kvb = w_mla_kvb[l].reshape(MLA_KV_RANK, MLA_HEADS, MLA_NOPE + MLA_DV)
    kpart = jnp.pad(kvb[:, :, :MLA_NOPE], ((0, 0), (0, 0), (0, HEAD_PAD - MLA_NOPE))).reshape(MLA_KV_RANK, -1)
    wkvb = jnp.concatenate([kpart, kvb[:, :, MLA_NOPE:].reshape(MLA_KV_RANK, -1)], axis=1).astype(BF16)

    def gain_rows(gv, nheads):
        plain = jnp.tile(gv[None, :], (1, nheads))
        swapped = _swap_halves(plain, nheads, GQA_DH)
        width = GQA_HEADS * GQA_DH
        return [jnp.pad(t, ((0, 0), (0, width - t.shape[1]))) for t in (plain, swapped)]

    gains = jnp.concatenate(gain_rows(g_gqa_q[l], GQA_HEADS) + gain_rows(g_gqa_k[l], GQA_KV_HEADS)
                            + [jnp.zeros((4, GQA_HEADS * GQA_DH), F32)], axis=0)
    jj = np.arange(GLA_WIDTH)
    bsum64 = jnp.asarray((jj[:, None] // GLA_DV) == (jj[None, :] // GLA_DV), BF16)
    wr = w_router[l].T
    wr_hi = wr.astype(BF16)
    wr_lo = (wr - wr_hi.astype(F32)).astype(BF16)
    return dict(w_in=w_in_r, wg=wg, bg=bg, g_qa=g_mla_qa[l][None, :], wqb=wqb, g_kva=g_mla_kva[l][None, :],
                wkvb=wkvb, gains=gains, g_gla=jnp.tile(g_gla_norm[l][None, :], (1, GLA_HEADS)),
                bsum64=bsum64, w_out=w_out[l].astype(BF16), ln1_g=ln1_g[l][None, :], ln1_b=ln1_b[l][None, :],
                wr_hi=wr_hi, wr_lo=wr_lo, ln2_g=ln2_g[l][None, :], ln2_b=ln2_b[l][None, :])


def _tables(l, with_pos):
    sm = MLA_SCALE * LOG2E
    sg = GQA_SCALE * LOG2E
    z = lambda n: jnp.zeros((l, n), F32)
    one = lambda n: jnp.ones((l, n), F32)
    if with_pos:
        t = jnp.arange(l, dtype=jnp.int32)
        row = (t // GRID_W).astype(F32)[:, None]
        col = (t % GRID_W).astype(F32)[:, None]

        def cs(nfreq):
            inv = ROPE_THETA ** (-jnp.arange(nfreq, dtype=F32) / nfreq)
            ar, ac = row * inv, col * inv
            c = jnp.concatenate([jnp.cos(ar), jnp.cos(ar), jnp.cos(ac), jnp.cos(ac)], axis=1)
            s = jnp.concatenate([-jnp.sin(ar), jnp.sin(ar), -jnp.sin(ac), jnp.sin(ac)], axis=1)
            return c, s

        c32, s32 = cs(MLA_ROPE // 4)
        c64, s64 = cs(GQA_DH // 4)
    else:
        c32, s32, c64, s64 = one(32), z(32), one(64), z(64)
    cm = jnp.concatenate([one(64), c32, z(32)], axis=1) * sm
    smm = jnp.concatenate([z(64), s32, z(32)], axis=1) * sm
    ck = jnp.concatenate([z(64), c32, z(32)], axis=1)
    sk = jnp.concatenate([z(64), s32, z(32)], axis=1)
    cg = jnp.concatenate([c64, c64], axis=1)
    sgg = jnp.concatenate([s64, s64], axis=1)
    return jnp.concatenate([cm, smm, ck, sk, cg * sg, sgg * sg, cg, sgg], axis=1)


def _heads_t(a, nheads, dh):
    b, l, _ = a.shape
    return a.reshape(b, l, nheads, dh).transpose(0, 2, 3, 1)


def _heads(a, nheads, dh):
    b, l, _ = a.shape
    return a.reshape(b, l, nheads, dh).transpose(0, 2, 1, 3)


def _attend(q, qg, k, kg, v):
    b, lq, _ = q.shape
    vt = _heads_t(v, N_KVHEADS, ATT_DV)
    extra = jnp.zeros(vt.shape[:2] + (ATT_DVP - ATT_DV, vt.shape[3]), BF16).at[:, :, 0, :].set(1.0)
    vt = jnp.concatenate([vt, extra], axis=2)
    ot_mla = _attn_call(_heads_t(q, MLA_HEADS, HEAD_PAD), _heads(k, MLA_HEADS, HEAD_PAD), vt[:, :MLA_HEADS], 1)
    ot_gqa = _attn_call(_heads_t(qg, GQA_HEADS, GQA_DH), _heads(kg, GQA_KV_HEADS, GQA_DH), vt[:, MLA_HEADS:],
                        GQA_GROUP)
    ot = jnp.concatenate([ot_mla, ot_gqa], axis=1)
    return ot.transpose(0, 3, 1, 2).reshape(b, lq, N_QHEADS * ATT_DV)


def _moe(h2, aff, x1, g2, w, layer, wexp):
    b, l, _ = h2.shape
    nc = l // LANE
    cap = EC_CAPACITY_FACTOR * l // N_EXPERTS
    slot, off = _topk_call(aff)
    s0 = off[:, :, 0].reshape(-1)
    eo = _expert_call(h2, slot, aff, s0, layer, *wexp)
    starts = s0.reshape(b, N_EXPERTS, nc)
    end = jnp.concatenate([starts[:, :, 1:], jnp.full((b, N_EXPERTS, 1), cap, jnp.int32)], axis=2).reshape(-1)
    return _combine_call(x1, eo, slot, s0, end, g2, w['ln2_g'], w['ln2_b'])


def kernel(x, c, ctx, c_ctx, w_mod, b_mod, w_in, w_gla_gate_f, b_gla_gate_f, w_gla_gate_b, b_gla_gate_b, g_gla_norm, g_mla_qa, w_mla_qb, g_mla_kva, w_mla_kvb, g_gqa_q, g_gqa_k, w_out, ln1_g, ln1_b, w_router, w_exp_gate, w_exp_up, w_exp_down, ln2_g, ln2_b):
    b, l, d = x.shape
    lc = ctx.shape[1]
    assert b < 8 and l % 256 == 0 and lc % 256 == 0
    cvec = jnp.zeros((8, d), F32).at[0:b].set(c).at[b].set(c_ctx)
    mods = _mod_call(cvec, w_mod, b_mod)
    tab_l = _tables(l, True)
    tab_c = _tables(lc, False)
    zero_state = jnp.zeros((b, GLA_WIDTH, 128), F32)
    cx = ctx
    for li in range(DEPTH):
        need_ctx = li < DEPTH - 1
        w = _prep_layer(li, w_in, w_gla_gate_f, b_gla_gate_f, w_gla_gate_b, b_gla_gate_b, g_gla_norm, g_mla_qa,
                        w_mla_qb, g_mla_kva, w_mla_kvb, g_gqa_q, g_gqa_k, w_out, ln1_g, ln1_b, w_router, ln2_g,
                        ln2_b)
        wexp = (w_exp_gate, w_exp_up, w_exp_down)
        sh1, sc1, g1, sh2, sc2, g2 = [m[:, None, :] for m in jnp.split(mods[li, 0:b], 6, axis=-1)]
        csh1, csc1, cg1, csh2, csc2, cg2 = [jnp.broadcast_to(m[:, None, :], (b, 1, d))
                                            for m in jnp.split(mods[li, b:b + 1], 6, axis=-1)]
        names = ('gq', 'gk', 'la', 'gv', 'gr', 'q', 'qg', 'k', 'kg', 'v')
        pl_ = dict(zip(names, _proj_call(x, sc1, sh1, tab_l, w)))
        pc_ = dict(zip(names, _proj_call(cx, csc1, csh1, tab_c, w)))
        ocf, s_f = _gla_call(pc_, zero_state, 0)
        ocb, s_b = _gla_call(pc_, zero_state, 1)
        olf, _ = _gla_call(pl_, s_f, 0)
        olb, _ = _gla_call(pl_, s_b, 1)
        both = lambda name: jnp.concatenate([pc_[name], pl_[name]], axis=1)
        oatt = _attend(pl_['q'], pl_['qg'], both('k'), both('kg'), both('v'))
        x1, h2, aff = _merge_call(x, olf, olb, pl_['gr'], oatt, g1, sc2, sh2, w)
        x = _moe(h2, aff, x1, g2, w, li, wexp)
        if need_ctx:
            oatt_c = _attend(pc_['q'], pc_['qg'], pc_['k'], pc_['kg'], pc_['v'])
            c1, ch2, caff = _merge_call(cx, ocf, ocb, pc_['gr'], oatt_c, cg1, csc2, csh2, w)
            cx = _moe(ch2, caff, c1, cg2, w, li, wexp)
    return x
```

```python
import functools
import math

import numpy as np
import jax
import jax.numpy as jnp
from jax import lax
from jax.experimental import pallas as pl
from jax.experimental.pallas import tpu as pltpu

F32 = jnp.float32
BF16 = jnp.bfloat16

D_MODEL = 1024
DEPTH = 4
GRID_W = 64
ROPE_THETA = 10000.0
LN_EPS = 1e-6
RMS_EPS = 1e-6
DEEPNORM_ALPHA = (2 * DEPTH) ** 0.25

GLA_HEADS, GLA_DK, GLA_DV, GLA_GATE_RANK, GLA_TAU, GLA_CHUNK = 4, 32, 64, 16, 16.0, 64
MLA_HEADS, MLA_Q_RANK, MLA_KV_RANK, MLA_NOPE, MLA_ROPE, MLA_DV = 6, 384, 256, 64, 32, 64
MLA_SCALE = (MLA_NOPE + MLA_ROPE) ** -0.5
GQA_HEADS, GQA_KV_HEADS, GQA_DH = 6, 2, 64
GQA_GROUP = GQA_HEADS // GQA_KV_HEADS
GQA_SCALE = GQA_DH ** -0.5
N_EXPERTS = 16
EC_CAPACITY_FACTOR = 2
D_EXPERT = 1024

GLA_WIDTH = GLA_HEADS * GLA_DV
IN_SPLITS = (GLA_HEADS * GLA_DK, GLA_HEADS * GLA_DK, GLA_WIDTH, GLA_WIDTH, GLA_GATE_RANK, GLA_GATE_RANK,
             MLA_Q_RANK, MLA_KV_RANK, MLA_ROPE,
             GQA_HEADS * GQA_DH, GQA_KV_HEADS * GQA_DH, GQA_KV_HEADS * GQA_DH)
IN_OFF = [0] + [int(o) for o in np.cumsum(IN_SPLITS)]

LANE = 128
HEAD_PAD = 128
N_QHEADS = MLA_HEADS + GQA_HEADS
N_KVHEADS = MLA_HEADS + GQA_KV_HEADS
ATT_DV = 64
ATT_DVP = 80
ATT_SUB = 128
ATT_AHEAD = 2
LOG2E = math.log2(math.e)
VMEM_LIMIT = 56 * 1024 * 1024

C_A, C_MQA, C_MKVA, C_AQ, C_AK, C_AV, C_G, C_AQS, C_AKS, C_GS, C_END = (
    0, 768, 1152, 1408, 1792, 1920, 2048, 2176, 2560, 2688, 2816)


def _cparams(sem):
    return pltpu.CompilerParams(dimension_semantics=sem, vmem_limit_bytes=VMEM_LIMIT)


def _dot(a, b):
    return jnp.dot(a, b, preferred_element_type=F32)


def _dot_nt(a, b):
    return lax.dot_general(a, b, (((1,), (1,)), ((), ())), preferred_element_type=F32)


def _dot_tn(a, b):
    return lax.dot_general(a, b, (((0,), (0,)), ((), ())), preferred_element_type=F32)


def _split2(x):
    hi = x.astype(BF16)
    lo = (x - hi.astype(F32)).astype(BF16)
    return hi, lo


def _split3(x):
    hi = x.astype(BF16)
    r = x - hi.astype(F32)
    mid = r.astype(BF16)
    lo = (r - mid.astype(F32)).astype(BF16)
    return hi, mid, lo


def _layer_norm(z, g, b):
    mu = jnp.mean(z, axis=-1, keepdims=True)
    zc = z - mu
    var = jnp.mean(zc * zc, axis=-1, keepdims=True)
    return zc * lax.rsqrt(var + LN_EPS) * g + b


def _mod_kernel(c_ref, w_ref, b_ref, o_ref):
    cv = c_ref[...]
    s = cv * (1.0 / (1.0 + jnp.exp(-cv)))
    o_ref[0] = _dot(s.astype(BF16), w_ref[0].astype(BF16)) + b_ref[0]


def _mod_call(cvec, w_mod, b_mod):
    depth, d, n6 = w_mod.shape
    bn = 768
    return pl.pallas_call(
        _mod_kernel,
        grid=(depth, n6 // bn),
        in_specs=[pl.BlockSpec((8, d), lambda l, j: (0, 0)),
                  pl.BlockSpec((1, d, bn), lambda l, j: (l, 0, j)),
                  pl.BlockSpec((1, 1, bn), lambda l, j: (l, 0, j))],
        out_specs=pl.BlockSpec((1, 8, bn), lambda l, j: (l, 0, j)),
        out_shape=jax.ShapeDtypeStruct((depth, 8, n6), F32),
        compiler_params=_cparams(("arbitrary", "arbitrary")),
    )(cvec, w_mod, b_mod.reshape(depth, 1, n6))


def _proj_kernel(x_ref, sc_ref, sh_ref, tab_ref, w_ref, wg_ref, bg_ref, gqa_ref, wqb_ref, gkva_ref, wkvb_ref,
                 gains_ref,
                 gq_ref, gk_ref, la_ref, gv_ref, gr_ref, q_ref, qg_ref, k_ref, kg_ref, v_ref):
    h = x_ref[0] * (1.0 + sc_ref[0]) + sh_ref[0]
    hb = h.astype(BF16)

    def mm(a, b):
        return _dot(hb, w_ref[:, a:b])

    def tab(i):
        return tab_ref[:, i * LANE:(i + 1) * LANE]

    a = mm(C_A, C_MQA)
    gq_ref[0] = a[:, 0:128] * (GLA_DK ** -0.5)
    gk_ref[0] = a[:, 128:256]
    gv_ref[0] = a[:, 256:512]
    gr_ref[0] = a[:, 512:768]

    g = mm(C_G, C_AQS)
    gs = mm(C_GS, C_END)
    lg = _dot(g.astype(BF16), wg_ref[...]) + bg_ref[...]
    la_ref[0] = (jnp.minimum(lg, 0.0) - jnp.log1p(jnp.exp(-jnp.abs(lg)))) * (1.0 / GLA_TAU)

    mqa = mm(C_MQA, C_MKVA)
    nq = mqa * lax.rsqrt(jnp.mean(mqa * mqa, axis=-1, keepdims=True) + RMS_EPS) * gqa_ref[...]
    q2 = _dot(nq.astype(BF16), wqb_ref[...])
    cm, sm = tab(0), tab(1)
    for hh in range(MLA_HEADS):
        lo = hh * HEAD_PAD
        qh = q2[:, lo:lo + HEAD_PAD] * cm + q2[:, 768 + lo:768 + lo + HEAD_PAD] * sm
        q_ref[0, :, lo:lo + HEAD_PAD] = qh.astype(BF16)

    mkva = mm(C_MKVA, C_AQ)
    nkv = mkva * lax.rsqrt(jnp.mean(mkva * mkva, axis=-1, keepdims=True) + RMS_EPS) * gkva_ref[...]
    kv = _dot(nkv.astype(BF16), wkvb_ref[...])
    kr = g * tab(2) + gs * tab(3)
    for hh in range(MLA_HEADS):
        lo = hh * HEAD_PAD
        k_ref[0, :, lo:lo + HEAD_PAD] = (kv[:, lo:lo + HEAD_PAD] + kr).astype(BF16)
    v_ref[0, :, 0:384] = kv[:, 768:1152].astype(BF16)

    first = lax.broadcasted_iota(jnp.int32, (1, LANE), 1) < GQA_DH

    def pair_rs(t):
        sq = t * t
        s_lo = jnp.sum(jnp.where(first, sq, 0.0), axis=-1, keepdims=True)
        s_hi = jnp.sum(jnp.where(first, 0.0, sq), axis=-1, keepdims=True)
        return lax.rsqrt(jnp.where(first, s_lo, s_hi) * (1.0 / GQA_DH) + RMS_EPS)

    def normed_rotary(x, xs, g_row, gs_row, cos, sin):
        rs = pair_rs(x)
        return x * rs * g_row * cos + xs * rs * gs_row * sin

    aq = mm(C_AQ, C_AK)
    aqs = mm(C_AQS, C_AKS)
    for pp in range(GQA_HEADS // 2):
        sl = slice(pp * LANE, (pp + 1) * LANE)
        qh = normed_rotary(aq[:, sl], aqs[:, sl], gains_ref[0:1, sl], gains_ref[1:2, sl], tab(4), tab(5))
        qg_ref[0, :, sl] = qh.astype(BF16)
    kh = normed_rotary(mm(C_AK, C_AV), mm(C_AKS, C_GS), gains_ref[2:3, 0:LANE], gains_ref[3:4, 0:LANE],
                       tab(6), tab(7))
    kg_ref[0] = kh.astype(BF16)
    v_ref[0, :, 384:512] = mm(C_AV, C_G).astype(BF16)


def _proj_call(x, sc, sh, tab, w):
    b, l, d = x.shape
    t = min(l, 512)
    full = lambda arr: pl.BlockSpec(arr.shape, lambda bi, i: (0,) * arr.ndim)
    row = lambda n: pl.BlockSpec((1, t, n), lambda bi, i: (bi, i, 0))
    vec = pl.BlockSpec((1, 1, d), lambda bi, i: (bi, 0, 0))
    consts = [w['w_in'], w['wg'], w['bg'], w['g_qa'], w['wqb'], w['g_kva'], w['wkvb'], w['gains']]
    outs = [(128, F32), (128, F32), (256, F32), (256, F32), (256, F32),
            (768, BF16), (384, BF16), (768, BF16), (128, BF16), (512, BF16)]
    return pl.pallas_call(
        _proj_kernel,
        grid=(b, l // t),
        in_specs=[row(d), vec, vec, pl.BlockSpec((t, 8 * LANE), lambda bi, i: (i, 0))] + [full(a) for a in consts],
        out_specs=[row(n) for n, _ in outs],
        out_shape=[jax.ShapeDtypeStruct((b, l, n), dt) for n, dt in outs],
        compiler_params=_cparams(("arbitrary", "arbitrary")),
    )(x, sc, sh, tab, *consts)


GLA_BLOCK = 256


def _gla_kernel(q_ref, k_ref, la_ref, v_ref, s0_ref, o_ref, sfin_ref, s_scr, *, reverse, nblk):
    i = pl.program_id(1)

    @pl.when(i == 0)
    def _():
        s_scr[...] = s0_ref[0]

    r = q_ref.shape[1]
    nch = r // GLA_CHUNK
    la = la_ref[0]
    q = q_ref[0]
    k = k_ref[0]
    v = v_ref[0]
    ri = lax.broadcasted_iota(jnp.int32, (r, r), 0)
    ci = lax.broadcasted_iota(jnp.int32, (r, r), 1)
    same = (ri >> 6) == (ci >> 6)
    tri = jnp.logical_and(same, (ci >= ri) if reverse else (ci <= ri))
    trib = jnp.where(tri, 1.0, 0.0).astype(BF16)
    oneb = jnp.where(same, 1.0, 0.0).astype(BF16)
    hi, mid, lo = _split3(la)
    bcum = _dot(trib, hi) + _dot(trib, mid) + _dot(trib, lo)
    tot = _dot(oneb, hi) + _dot(oneb, mid) + _dot(oneb, lo)
    qd = q * jnp.exp(bcum)
    kinv = (k * jnp.exp(-bcum)).astype(BF16)
    kend = (k * jnp.exp(tot - bcum)).astype(BF16)
    dec = jnp.exp(tot)
    vb = v.astype(BF16)
    lane_k = lax.broadcasted_iota(jnp.int32, (1, GLA_HEADS * GLA_DK), 1)
    lane_v = lax.broadcasted_iota(jnp.int32, (1, GLA_WIDTH), 1)
    o = jnp.zeros((r, GLA_WIDTH), F32)
    for hh in range(GLA_HEADS):
        qh = jnp.where((lane_k >> 5) == hh, qd, 0.0).astype(BF16)
        att = jnp.where(tri, _dot_nt(qh, kinv), 0.0)
        vh = jnp.where((lane_v >> 6) == hh, v, 0.0).astype(BF16)
        o = o + _dot(att.astype(BF16), vh)
    qdb = qd.astype(BF16)
    bd = (lax.broadcasted_iota(jnp.int32, (GLA_WIDTH, GLA_HEADS * GLA_DK), 0) >> 6) == (
        lax.broadcasted_iota(jnp.int32, (GLA_WIDTH, GLA_HEADS * GLA_DK), 1) >> 5)
    s = s_scr[...]
    order = range(nch - 1, -1, -1) if reverse else range(nch)
    for cc in order:
        rows = slice(cc * GLA_CHUNK, (cc + 1) * GLA_CHUNK)
        o_ref[0, rows, :] = o[rows] + _dot_nt(qdb[rows], s.astype(BF16))
        ut = _dot_tn(vb[rows], kend[rows])
        s = dec[cc * GLA_CHUNK:cc * GLA_CHUNK + 1, :] * s + jnp.where(bd, ut, 0.0)
    s_scr[...] = s

    @pl.when(i == nblk - 1)
    def _():
        sfin_ref[0] = s


def _gla_call(p, s0, direction):
    b, l, _ = p['gq'].shape
    r = min(l, GLA_BLOCK)
    nblk = l // r
    reverse = direction == 1
    blk = (lambda i: nblk - 1 - i) if reverse else (lambda i: i)
    row = lambda n, col=0: pl.BlockSpec((1, r, n), lambda bi, i: (bi, blk(i), col))
    st = pl.BlockSpec((1, GLA_WIDTH, 128), lambda bi, i: (bi, 0, 0))
    return pl.pallas_call(
        functools.partial(_gla_kernel, reverse=reverse, nblk=nblk),
        grid=(b, nblk),
        in_specs=[row(128), row(128), row(128, direction), row(256), st],
        out_specs=[row(256), st],
        out_shape=[jax.ShapeDtypeStruct((b, l, GLA_WIDTH), F32), jax.ShapeDtypeStruct((b, GLA_WIDTH, 128), F32)],
        scratch_shapes=[pltpu.VMEM((GLA_WIDTH, 128), F32)],
        compiler_params=_cparams(("arbitrary", "arbitrary")),
    )(p['gq'], p['gk'], p['la'], p['gv'], s0)


def _attn_kernel(qt_ref, *refs, nseg):
    seg_refs, (o_ref, m_scr, acc_scr) = refs[:2 * nseg], refs[2 * nseg:]
    qt = qt_ref[0, 0]
    subs = [(seg_refs[2 * g], seg_refs[2 * g + 1], c)
            for g in range(nseg) for c in range(seg_refs[2 * g].shape[2] // ATT_SUB)]
    nsub = len(subs)
    m = jnp.full(m_scr.shape, -jnp.inf, F32)
    acc = jnp.zeros(acc_scr.shape, F32)

    def scores(i):
        k_ref, _, c = subs[i]
        return _dot(k_ref[0, 0, c * ATT_SUB:(c + 1) * ATT_SUB, :], qt)

    def weighted_values(acc_in, alpha_c, p_c, i):
        _, vt_ref, c = subs[i]
        return alpha_c * acc_in + _dot(vt_ref[0, 0, :, c * ATT_SUB:(c + 1) * ATT_SUB], p_c)

    ahead = [scores(i) for i in range(min(ATT_AHEAD, nsub))]
    pending = None
    for i in range(nsub):
        s = ahead.pop(0)
        if i + ATT_AHEAD < nsub:
            ahead.append(scores(i + ATT_AHEAD))
        m_new = jnp.maximum(m, jnp.max(s, axis=0, keepdims=True))
        alpha = jnp.exp2(m - m_new)
        p = jnp.exp2(s - m_new).astype(BF16)
        m = m_new
        if pending is not None:
            acc = weighted_values(acc, *pending)
        pending = (alpha, p, i)
    acc = weighted_values(acc, *pending)
    o_ref[0, 0] = (acc[0:ATT_DV, :] / acc[ATT_DV:ATT_DV + 1, :]).astype(o_ref.dtype)


def _pick_tile(n, options):
    for t in options:
        if n % t == 0:
            return t
    raise ValueError(f"no tile for {n}")


def _attn_call(qt, segments, group):
    b, nh, dk, lq = qt.shape
    tq = _pick_tile(lq, (1024, 512, 256))
    in_specs = [pl.BlockSpec((1, 1, dk, tq), lambda bi, h, qi: (bi, h, 0, qi))]
    operands = [qt]
    for k, vt in segments:
        lk = k.shape[2]
        in_specs += [pl.BlockSpec((1, 1, lk, dk), lambda bi, h, qi: (bi, h // group, 0, 0)),
                     pl.BlockSpec((1, 1, ATT_DVP, lk), lambda bi, h, qi: (bi, h // group, 0, 0))]
        operands += [k, vt]
    return pl.pallas_call(
        functools.partial(_attn_kernel, nseg=len(segments)),
        grid=(b, nh, lq // tq),
        in_specs=in_specs,
        out_specs=pl.BlockSpec((1, 1, ATT_DV, tq), lambda bi, h, qi: (bi, h, 0, qi)),
        out_shape=jax.ShapeDtypeStruct((b, nh, ATT_DV, lq), BF16),
        scratch_shapes=[pltpu.VMEM((1, tq), F32), pltpu.VMEM((ATT_DVP, tq), F32)],
        compiler_params=_cparams(("arbitrary", "arbitrary", "arbitrary")),
    )(*operands)


def _merge_kernel(x_ref, of_ref, ob_ref, gr_ref, oa_ref, g1_ref, sc_ref, sh_ref, ggla_ref, bs_ref, wout_ref,
                  lng_ref, lnb_ref, wrh_ref, wrl_ref, x1_ref, h2_ref, aff_ref):
    o = of_ref[0] + ob_ref[0]
    hi, lo = _split2(o * o)
    ms = (_dot(hi, bs_ref[...]) + _dot(lo, bs_ref[...])) * (1.0 / GLA_DV)
    r = gr_ref[0]
    gla = o * lax.rsqrt(ms + RMS_EPS) * ggla_ref[...] * (r * (1.0 / (1.0 + jnp.exp(-r))))
    a = _dot(gla.astype(BF16), wout_ref[0:GLA_WIDTH, :]) + _dot(oa_ref[0], wout_ref[GLA_WIDTH:, :])
    x1 = _layer_norm(DEEPNORM_ALPHA * x_ref[0] + g1_ref[0] * a, lng_ref[...], lnb_ref[...])
    x1_ref[0] = x1
    h2 = x1 * (1.0 + sc_ref[0]) + sh_ref[0]
    h2h, h2l = _split2(h2)
    h2_ref[0] = h2h
    wh, wl = wrh_ref[...], wrl_ref[...]
    logits = _dot_nt(wh, h2h) + _dot_nt(wh, h2l) + _dot_nt(wl, h2h)
    e = jnp.exp(logits - jnp.max(logits, axis=0, keepdims=True))
    aff_ref[0] = e / jnp.sum(e, axis=0, keepdims=True)


def _merge_call(x, of, ob, gr, oatt, g1, sc2, sh2, w):
    b, l, d = x.shape
    t = min(l, 512)
    full = lambda arr: pl.BlockSpec(arr.shape, lambda bi, i: (0,) * arr.ndim)
    row = lambda n: pl.BlockSpec((1, t, n), lambda bi, i: (bi, i, 0))
    vec = pl.BlockSpec((1, 1, d), lambda bi, i: (bi, 0, 0))
    consts = [w['g_gla'], w['bsum64'], w['w_out'], w['ln1_g'], w['ln1_b'], w['wr_hi'], w['wr_lo']]
    return pl.pallas_call(
        _merge_kernel,
        grid=(b, l // t),
        in_specs=[row(d), row(256), row(256), row(256), row(768), vec, vec, vec] + [full(a) for a in consts],
        out_specs=[row(d), row(d), pl.BlockSpec((1, N_EXPERTS, t), lambda bi, i: (bi, 0, i))],
        out_shape=[jax.ShapeDtypeStruct((b, l, d), F32), jax.ShapeDtypeStruct((b, l, d), BF16),
                   jax.ShapeDtypeStruct((b, N_EXPERTS, l), F32)],
        compiler_params=_cparams(("arbitrary", "arbitrary")),
    )(x, of, ob, gr, oatt, g1, sc2, sh2, *consts)


def _topk_kernel(aff_ref, slot_ref, off_ref, *, nc, cap):
    shift = int(math.log2(nc))
    bits = pltpu.bitcast(aff_ref[0], jnp.int32)
    r = bits.shape[0]
    ri = lax.broadcasted_iota(jnp.int32, (r, r), 0)
    ci = lax.broadcasted_iota(jnp.int32, (r, r), 1)
    same = (ri >> shift) == (ci >> shift)
    ones_bd = jnp.where(same, 1.0, 0.0).astype(BF16)
    lstrict = jnp.where(jnp.logical_and(same, ci < ri), 1.0, 0.0).astype(BF16)
    li = lax.broadcasted_iota(jnp.int32, (LANE, LANE), 0)
    lj = lax.broadcasted_iota(jnp.int32, (LANE, LANE), 1)
    uex = jnp.where(li < lj, 1.0, 0.0).astype(BF16)
    ones = jnp.ones((LANE, LANE), BF16)

    def total(mask):
        colsum = _dot(ones_bd, jnp.where(mask, 1.0, 0.0).astype(BF16))
        return _dot(colsum.astype(BF16), ones)

    def excl_cumsum(mask):
        mb = jnp.where(mask, 1.0, 0.0).astype(BF16)
        off = _dot(lstrict, _dot(mb, ones).astype(BF16))
        return _dot(mb, uex) + off, off

    def body(it, v):
        cand = v | lax.shift_left(jnp.int32(1), jnp.int32(30) - it)
        return jnp.where(total(bits >= cand) >= cap, cand, v)

    v = lax.fori_loop(0, 31, body, jnp.zeros(bits.shape, jnp.int32))
    gt = bits > v
    eq = bits == v
    need = cap - total(gt)
    rank_eq, _ = excl_cumsum(eq)
    sel = jnp.logical_or(gt, jnp.logical_and(eq, rank_eq < need))
    pos, off = excl_cumsum(sel)
    slot_ref[0] = jnp.where(sel, pos.astype(jnp.int32), -1)
    off_ref[0] = off.astype(jnp.int32)


def _topk_call(aff):
    b, e, l = aff.shape
    nc = l // LANE
    cap = EC_CAPACITY_FACTOR * l // N_EXPERTS
    r = e * nc
    spec = pl.BlockSpec((1, r, LANE), lambda bi: (bi, 0, 0))
    return pl.pallas_call(
        functools.partial(_topk_kernel, nc=nc, cap=cap),
        grid=(b,),
        in_specs=[spec],
        out_specs=[spec, spec],
        out_shape=[jax.ShapeDtypeStruct((b, r, LANE), jnp.int32)] * 2,
        compiler_params=_cparams(("arbitrary",)),
    )(aff.reshape(b, r, LANE))


def _expert_kernel(s0_ref, h_ref, slot_ref, aff_ref, wg_ref, wu_ref, wd_ref, o_ref, xs_scr, gate_scr,
                   *, nb, nsub, nc, cap, win):
    e, b, j = pl.program_id(0), pl.program_id(1), pl.program_id(2)

    @pl.when(j == 0)
    def _():
        xs_scr[...] = jnp.zeros(xs_scr.shape, F32)
        gate_scr[...] = jnp.zeros(gate_scr.shape, F32)

    base = (b * N_EXPERTS + e) * nc + j * nsub
    rows = lax.broadcasted_iota(jnp.int32, (win, LANE), 0)
    for u in range(nsub):
        s0a = pl.multiple_of((s0_ref[base + u] >> 3) << 3, 8)
        lanes = slice(u * LANE, (u + 1) * LANE)
        hit = (rows + s0a) == slot_ref[0, :, lanes]
        comp = _dot(jnp.where(hit, 1.0, 0.0).astype(BF16), h_ref[0, lanes, :])
        xs_scr[pl.ds(s0a, win), :] += comp
        gate_scr[pl.ds(s0a, win), :] += jnp.sum(jnp.where(hit, aff_ref[0, :, lanes], 0.0), axis=1, keepdims=True)

    @pl.when(j == nb - 1)
    def _():
        rc = min(cap, 256)
        wg, wu, wd = (r[0, 0].astype(BF16) for r in (wg_ref, wu_ref, wd_ref))
        for c0 in range(0, cap, rc):
            xs = xs_scr[c0:c0 + rc, :].astype(BF16)
            gg = _dot(xs, wg)
            hid = (gg * (1.0 / (1.0 + jnp.exp(-gg)))) * _dot(xs, wu)
            out = _dot(hid.astype(BF16), wd) * gate_scr[c0:c0 + rc, :]
            o_ref[0, 0, c0:c0 + rc, :] = out.astype(BF16)
        o_ref[0, 0, cap:, :] = jnp.zeros((o_ref.shape[2] - cap, o_ref.shape[3]), BF16)


def _expert_call(h2, slot, aff, s0, layer, wg, wu, wd):
    b, l, d = h2.shape
    nc = l // LANE
    cap = EC_CAPACITY_FACTOR * l // N_EXPERTS
    tb = min(l, 2048)
    nsub = tb // LANE
    nb = l // tb
    win = LANE + 8
    f = wg.shape[3]
    row3 = lambda e, bi, j, s: ((bi * N_EXPERTS + e) * nb + j, 0, 0)
    grid_spec = pltpu.PrefetchScalarGridSpec(
        num_scalar_prefetch=1,
        grid=(N_EXPERTS, b, nb),
        in_specs=[pl.BlockSpec((1, tb, d), lambda e, bi, j, s: (bi, j, 0)),
                  pl.BlockSpec((1, 1, tb), row3),
                  pl.BlockSpec((1, 1, tb), row3),
                  pl.BlockSpec((1, 1, d, f), lambda e, bi, j, s: (layer, e, 0, 0)),
                  pl.BlockSpec((1, 1, d, f), lambda e, bi, j, s: (layer, e, 0, 0)),
                  pl.BlockSpec((1, 1, f, d), lambda e, bi, j, s: (layer, e, 0, 0))],
        out_specs=pl.BlockSpec((1, 1, cap + COMBINE_WIN, d), lambda e, bi, j, s: (bi, e, 0, 0)),
        scratch_shapes=[pltpu.VMEM((cap + win, d), F32), pltpu.VMEM((cap + win, 1), F32)],
    )
    return pl.pallas_call(
        functools.partial(_expert_kernel, nb=nb, nsub=nsub, nc=nc, cap=cap, win=win),
        grid_spec=grid_spec,
        out_shape=jax.ShapeDtypeStruct((b, N_EXPERTS, cap + COMBINE_WIN, d), BF16),
        compiler_params=_cparams(("arbitrary", "arbitrary", "arbitrary")),
    )(s0, h2, slot.reshape(b * N_EXPERTS * nb, 1, tb), aff.reshape(b * N_EXPERTS * nb, 1, tb), wg, wu, wd)


COMBINE_WIN = LANE + 16


COMBINE_BASE = 48
COMBINE_EXT = COMBINE_WIN - COMBINE_BASE


def _combine_kernel(s0_ref, end_ref, eo_hbm, slot_ref, x1_ref, g2_ref, lng_ref, lnb_ref, o_ref,
                    base, ext, sem, *, nc, nsteps):
    b, j = pl.program_id(0), pl.program_id(1)
    t = b * nc + j
    cur = t % 2

    def win_start(bb, jj, e):
        return pl.multiple_of((s0_ref[(bb * N_EXPERTS + e) * nc + jj] >> 4) << 4, 16)

    def needs_ext(bb, jj, e):
        return end_ref[(bb * N_EXPERTS + e) * nc + jj] > win_start(bb, jj, e) + COMBINE_BASE

    def base_copies(bb, jj, sl):
        out = []
        for e in range(N_EXPERTS):
            rows = pl.ds(win_start(bb, jj, e), COMBINE_BASE)
            dst = pl.ds(e * COMBINE_BASE, COMBINE_BASE)
            out.append(pltpu.make_async_copy(eo_hbm.at[bb, e, rows, :], base.at[sl, dst, :], sem.at[sl, 0]))
        return out

    def ext_copies(bb, jj, sl, e):
        rows = pl.ds(win_start(bb, jj, e) + COMBINE_BASE, COMBINE_EXT)
        dst = pl.ds(e * COMBINE_EXT, COMBINE_EXT)
        return [pltpu.make_async_copy(eo_hbm.at[bb, e, rows, :], ext.at[sl, dst, :], sem.at[sl, 1])]

    def start_all(bb, jj, sl):
        for cp in base_copies(bb, jj, sl):
            cp.start()
        for e in range(N_EXPERTS):
            @pl.when(needs_ext(bb, jj, e))
            def _():
                for cp in ext_copies(bb, jj, sl, e):
                    cp.start()

    @pl.when(t == 0)
    def _():
        ext[...] = jnp.zeros(ext.shape, BF16)
        start_all(b, j, cur)

    @pl.when(t + 1 < nsteps)
    def _():
        wrap = j + 1 == nc
        start_all(jnp.where(wrap, b + 1, b), jnp.where(wrap, 0, j + 1), 1 - cur)

    for cp in base_copies(b, j, cur):
        cp.wait()
    any_ext = needs_ext(b, j, 0)
    for e in range(N_EXPERTS):
        any_ext = jnp.logical_or(any_ext, needs_ext(b, j, e))

        @pl.when(needs_ext(b, j, e))
        def _():
            for cp in ext_copies(b, j, cur, e):
                cp.wait()

    def onehot(nrows, offset):
        rows = lax.broadcasted_iota(jnp.int32, (nrows, LANE), 0) + offset
        return jnp.concatenate(
            [jnp.where((rows + win_start(b, j, e)) == slot_ref[0, e:e + 1, :], 1.0, 0.0).astype(BF16)
             for e in range(N_EXPERTS)], axis=0)

    p = onehot(COMBINE_BASE, 0)
    z = DEEPNORM_ALPHA * x1_ref[0] + g2_ref[0] * _dot_tn(p, base[cur])
    o_ref[0] = _layer_norm(z, lng_ref[...], lnb_ref[...])

    @pl.when(any_ext)
    def _():
        pe = onehot(COMBINE_EXT, COMBINE_BASE)
        z2 = z + g2_ref[0] * _dot_tn(pe, ext[cur])
        o_ref[0] = _layer_norm(z2, lng_ref[...], lnb_ref[...])


def _combine_call(x1, eo, slot, s0, end, g2, lng, lnb):
    b, l, d = x1.shape
    nc = l // LANE
    full = lambda arr: pl.BlockSpec(arr.shape, lambda bi, j, s, en: (0,) * arr.ndim)
    grid_spec = pltpu.PrefetchScalarGridSpec(
        num_scalar_prefetch=2,
        grid=(b, nc),
        in_specs=[pl.BlockSpec(memory_space=pl.ANY),
                  pl.BlockSpec((1, N_EXPERTS, LANE), lambda bi, j, s, en: (bi, 0, j)),
                  pl.BlockSpec((1, LANE, d), lambda bi, j, s, en: (bi, j, 0)),
                  pl.BlockSpec((1, 1, d), lambda bi, j, s, en: (bi, 0, 0)),
                  full(lng), full(lnb)],
        out_specs=pl.BlockSpec((1, LANE, d), lambda bi, j, s, en: (bi, j, 0)),
        scratch_shapes=[pltpu.VMEM((2, N_EXPERTS * COMBINE_BASE, d), BF16),
                        pltpu.VMEM((2, N_EXPERTS * COMBINE_EXT, d), BF16),
                        pltpu.SemaphoreType.DMA((2, 2))],
    )
    return pl.pallas_call(
        functools.partial(_combine_kernel, nc=nc, nsteps=b * nc),
        grid_spec=grid_spec,
        out_shape=jax.ShapeDtypeStruct((b, l, d), F32),
        compiler_params=_cparams(("arbitrary", "arbitrary")),
    )(s0, end, eo, slot.reshape(b, N_EXPERTS, l), x1, g2, lng, lnb)


def _swap_halves(w, nheads, dh):
    q = dh // 4
    return jnp.flip(w.reshape(w.shape[0], nheads, 2, 2, q), axis=3).reshape(w.shape[0], nheads * dh)


def _prep_layer(l, w_in, w_gla_gate_f, b_gla_gate_f, w_gla_gate_b, b_gla_gate_b, g_gla_norm, g_mla_qa, w_mla_qb,
                g_mla_kva, w_mla_kvb, g_gqa_q, g_gqa_k, w_out, ln1_g, ln1_b, w_router, ln2_g, ln2_b):
    wi = w_in[l]
    d = wi.shape[0]
    o = IN_OFF
    aq, ak, av = wi[:, o[9]:o[10]], wi[:, o[10]:o[11]], wi[:, o[11]:o[12]]
    mkr = wi[:, o[8]:o[9]]
    zeros = lambda n: jnp.zeros((d, n), F32)
    g_blk = jnp.concatenate([wi[:, o[4]:o[6]], zeros(32), mkr, zeros(32)], axis=1)
    gs_blk = jnp.concatenate([zeros(64), _swap_halves(mkr, 1, MLA_ROPE), zeros(32)], axis=1)
    w_in_r = jnp.concatenate([
        wi[:, 0:o[4]], wi[:, o[6]:o[7]], wi[:, o[7]:o[8]],
        aq, ak, av, g_blk,
        _swap_halves(aq, GQA_HEADS, GQA_DH), _swap_halves(ak, GQA_KV_HEADS, GQA_DH), gs_blk], axis=1).astype(BF16)
    assert w_in_r.shape[1] == C_END

    r = GLA_GATE_RANK
    wg = jnp.zeros((LANE, 256), F32)
    wg = wg.at[0:r, 0:128].set(w_gla_gate_f[l]).at[r:2 * r, 128:256].set(w_gla_gate_b[l]).astype(BF16)
    bg = jnp.concatenate([b_gla_gate_f[l], b_gla_gate_b[l]])[None, :]

    qb = w_mla_qb[l].reshape(MLA_Q_RANK, MLA_HEADS, MLA_NOPE + MLA_ROPE)
    qb_rope_sw = _swap_halves(qb[:, :, MLA_NOPE:].reshape(MLA_Q_RANK, -1), MLA_HEADS, MLA_ROPE).reshape(
        MLA_Q_RANK, MLA_HEADS, MLA_ROPE)
    qb_sw = jnp.concatenate([jnp.zeros_like(qb[:, :, :MLA_NOPE]), qb_rope_sw], axis=2)
    padq = lambda t: jnp.pad(t, ((0, 0), (0, 0), (0, HEAD_PAD - t.shape[2]))).reshape(MLA_Q_RANK, -1)
    wqb = jnp.concatenate([padq(qb), padq(qb_sw)], axis=1).astype(BF16)

    kvb = w_mla_kvb[l].reshape(MLA_KV_RANK, MLA_HEADS, MLA_NOPE + MLA_DV)
    kpart = jnp.pad(kvb[:, :, :MLA_NOPE], ((0, 0), (0, 0), (0, HEAD_PAD - MLA_NOPE))).reshape(MLA_KV_RANK, -1)
    wkvb = jnp.concatenate([kpart, kvb[:, :, MLA_NOPE:].reshape(MLA_KV_RANK, -1)], axis=1).astype(BF16)

    def gain_rows(gv, nheads):
        plain = jnp.tile(gv[None, :], (1, nheads))
        swapped = _swap_halves(plain, nheads, GQA_DH)
        width = GQA_HEADS * GQA_DH
        return [jnp.pad(t, ((0, 0), (0, width - t.shape[1]))) for t in (plain, swapped)]

    gains = jnp.concatenate(gain_rows(g_gqa_q[l], GQA_HEADS) + gain_rows(g_gqa_k[l], GQA_KV_HEADS)
                            + [jnp.zeros((4, GQA_HEADS * GQA_DH), F32)], axis=0)
    jj = np.arange(GLA_WIDTH)
    bsum64 = jnp.asarray((jj[:, None] // GLA_DV) == (jj[None, :] // GLA_DV), BF16)
    wr = w_router[l].T
    wr_hi = wr.astype(BF16)
    wr_lo = (wr - wr_hi.astype(F32)).astype(BF16)
    return dict(w_in=w_in_r, wg=wg, bg=bg, g_qa=g_mla_qa[l][None, :], wqb=wqb, g_kva=g_mla_kva[l][None, :],
                wkvb=wkvb, gains=gains, g_gla=jnp.tile(g_gla_norm[l][None, :], (1, GLA_HEADS)),
                bsum64=bsum64, w_out=w_out[l].astype(BF16), ln1_g=ln1_g[l][None, :], ln1_b=ln1_b[l][None, :],
                wr_hi=wr_hi, wr_lo=wr_lo, ln2_g=ln2_g[l][None, :], ln2_b=ln2_b[l][None, :])


def _tables(l, with_pos):
    sm = MLA_SCALE * LOG2E
    sg = GQA_SCALE * LOG2E
    z = lambda n: jnp.zeros((l, n), F32)
    one = lambda n: jnp.ones((l, n), F32)
    if with_pos:
        t = jnp.arange(l, dtype=jnp.int32)
        row = (t // GRID_W).astype(F32)[:, None]
        col = (t % GRID_W).astype(F32)[:, None]

        def cs(nfreq):
            inv = ROPE_THETA ** (-jnp.arange(nfreq, dtype=F32) / nfreq)
            ar, ac = row * inv, col * inv
            c = jnp.concatenate([jnp.cos(ar), jnp.cos(ar), jnp.cos(ac), jnp.cos(ac)], axis=1)
            s = jnp.concatenate([-jnp.sin(ar), jnp.sin(ar), -jnp.sin(ac), jnp.sin(ac)], axis=1)
            return c, s

        c32, s32 = cs(MLA_ROPE // 4)
        c64, s64 = cs(GQA_DH // 4)
    else:
        c32, s32, c64, s64 = one(32), z(32), one(64), z(64)
    cm = jnp.concatenate([one(64), c32, z(32)], axis=1) * sm
    smm = jnp.concatenate([z(64), s32, z(32)], axis=1) * sm
    ck = jnp.concatenate([z(64), c32, z(32)], axis=1)
    sk = jnp.concatenate([z(64), s32, z(32)], axis=1)
    cg = jnp.concatenate([c64, c64], axis=1)
    sgg = jnp.concatenate([s64, s64], axis=1)
    return jnp.concatenate([cm, smm, ck, sk, cg * sg, sgg * sg, cg, sgg], axis=1)


def _heads_t(a, nheads, dh):
    b, l, _ = a.shape
    return a.reshape(b, l, nheads, dh).transpose(0, 2, 3, 1)


def _heads(a, nheads, dh):
    b, l, _ = a.shape
    return a.reshape(b, l, nheads, dh).transpose(0, 2, 1, 3)


def _attend(q, qg, parts):
    b, lq, _ = q.shape
    mla, gqa = [], []
    for p in parts:
        vt = _heads_t(p['v'], N_KVHEADS, ATT_DV)
        extra = jnp.zeros(vt.shape[:2] + (ATT_DVP - ATT_DV, vt.shape[3]), BF16).at[:, :, 0, :].set(1.0)
        vt = jnp.concatenate([vt, extra], axis=2)
        mla.append((_heads(p['k'], MLA_HEADS, HEAD_PAD), vt[:, :MLA_HEADS]))
        gqa.append((_heads(p['kg'], GQA_KV_HEADS, GQA_DH), vt[:, MLA_HEADS:]))
    ot_mla = _attn_call(_heads_t(q, MLA_HEADS, HEAD_PAD), mla, 1)
    ot_gqa = _attn_call(_heads_t(qg, GQA_HEADS, GQA_DH), gqa, GQA_GROUP)
    ot = jnp.concatenate([ot_mla, ot_gqa], axis=1)
    return ot.transpose(0, 3, 1, 2).reshape(b, lq, N_QHEADS * ATT_DV)


def _moe(h2, aff, x1, g2, w, layer, wexp):
    b, l, _ = h2.shape
    nc = l // LANE
    cap = EC_CAPACITY_FACTOR * l // N_EXPERTS
    slot, off = _topk_call(aff)
    s0 = off[:, :, 0].reshape(-1)
    eo = _expert_call(h2, slot, aff, s0, layer, *wexp)
    starts = s0.reshape(b, N_EXPERTS, nc)
    end = jnp.concatenate([starts[:, :, 1:], jnp.full((b, N_EXPERTS, 1), cap, jnp.int32)], axis=2).reshape(-1)
    return _combine_call(x1, eo, slot, s0, end, g2, w['ln2_g'], w['ln2_b'])


def kernel(x, c, ctx, c_ctx, w_mod, b_mod, w_in, w_gla_gate_f, b_gla_gate_f, w_gla_gate_b, b_gla_gate_b, g_gla_norm, g_mla_qa, w_mla_qb, g_mla_kva, w_mla_kvb, g_gqa_q, g_gqa_k, w_out, ln1_g, ln1_b, w_router, w_exp_gate, w_exp_up, w_exp_down, ln2_g, ln2_b):
    b, l, d = x.shape
    lc = ctx.shape[1]
    assert b < 8 and l % 256 == 0 and lc % 256 == 0
    cvec = jnp.zeros((8, d), F32).at[0:b].set(c).at[b].set(c_ctx)
    mods = _mod_call(cvec, w_mod, b_mod)
    tab_l = _tables(l, True)
    tab_c = _tables(lc, False)
    zero_state = jnp.zeros((b, GLA_WIDTH, 128), F32)
    cx = ctx
    for li in range(DEPTH):
        need_ctx = li < DEPTH - 1
        w = _prep_layer(li, w_in, w_gla_gate_f, b_gla_gate_f, w_gla_gate_b, b_gla_gate_b, g_gla_norm, g_mla_qa,
                        w_mla_qb, g_mla_kva, w_mla_kvb, g_gqa_q, g_gqa_k, w_out, ln1_g, ln1_b, w_router, ln2_g,
                        ln2_b)
        wexp = (w_exp_gate, w_exp_up, w_exp_down)
        sh1, sc1, g1, sh2, sc2, g2 = [m[:, None, :] for m in jnp.split(mods[li, 0:b], 6, axis=-1)]
        csh1, csc1, cg1, csh2, csc2, cg2 = [jnp.broadcast_to(m[:, None, :], (b, 1, d))
                                            for m in jnp.split(mods[li, b:b + 1], 6, axis=-1)]
        names = ('gq', 'gk', 'la', 'gv', 'gr', 'q', 'qg', 'k', 'kg', 'v')
        pl_ = dict(zip(names, _proj_call(x, sc1, sh1, tab_l, w)))
        pc_ = dict(zip(names, _proj_call(cx, csc1, csh1, tab_c, w)))
        ocf, s_f = _gla_call(pc_, zero_state, 0)
        ocb, s_b = _gla_call(pc_, zero_state, 1)
        olf, _ = _gla_call(pl_, s_f, 0)
        olb, _ = _gla_call(pl_, s_b, 1)
        oatt = _attend(pl_['q'], pl_['qg'], [pc_, pl_])
        x1, h2, aff = _merge_call(x, olf, olb, pl_['gr'], oatt, g1, sc2, sh2, w)
        x = _moe(h2, aff, x1, g2, w, li, wexp)
        if need_ctx:
            oatt_c = _attend(pc_['q'], pc_['qg'], [pc_])
            c1, ch2, caff = _merge_call(cx, ocf, ocb, pc_['gr'], oatt_c, cg1, csc2, csh2, w)
            cx = _moe(ch2, caff, c1, cg2, w, li, wexp)
    return x
```
